```python
import jax, jax.numpy as jnp
from jax import lax
import numpy as np

D_MODEL = 1024
BATCH = 16
SEQ = 2048
DEPTH = 1

CHUNK = 64
ROPE_THETA = 10000.0
RMS_EPS = 1e-6
NEG_INF = -1e30

A_HEADS = 8
A_HEAD_DIM = 64
A_WIDTH = A_HEADS * A_HEAD_DIM
IDX_HEADS = 8
IDX_DIM = 64
IDX_ROPE_DIM = 32
TOPK_MAX = 256
SPARSE_Q_BLOCK = 32

B_HEADS = 8
B_NOPE_DIM = 64
B_ROPE_DIM = 32
B_QK_DIM = B_NOPE_DIM + B_ROPE_DIM
B_V_DIM = 64
B_WIDTH = B_HEADS * B_V_DIM
Q_LORA = 384
KV_LORA = 256
DENSE_Q_BLOCK = 128

D_MIX = A_WIDTH + B_WIDTH
IN_SPLITS = (A_WIDTH, A_WIDTH, A_WIDTH, A_WIDTH, IDX_HEADS * IDX_DIM, IDX_DIM, IDX_HEADS,
             Q_LORA, KV_LORA, B_ROPE_DIM, B_WIDTH)
D_IN = 4 * A_WIDTH + IDX_HEADS * IDX_DIM + IDX_DIM + IDX_HEADS + Q_LORA + KV_LORA + B_ROPE_DIM + B_WIDTH

kernel_name = "hybrid_dsa_mla_parallel_heads"


def rms_norm(x, g):
    xf = x.astype(jnp.float32)
    y = xf * lax.rsqrt(jnp.mean(xf * xf, axis=-1, keepdims=True) + RMS_EPS)
    return (y * g.astype(jnp.float32)).astype(x.dtype)


def rope(x, pos):
    d = x.shape[-1]
    half = d // 2
    freqs = jnp.power(ROPE_THETA, -jnp.arange(half, dtype=jnp.float32) * 2.0 / d)
    ang = pos.astype(jnp.float32)[:, None] * freqs[None, :]
    cos = jnp.cos(ang)[None, :, None, :]
    sin = jnp.sin(ang)[None, :, None, :]
    xf = x.astype(jnp.float32)
    x1, x2 = xf[..., :half], xf[..., half:]
    return jnp.concatenate([x1 * cos - x2 * sin, x2 * cos + x1 * sin], axis=-1).astype(x.dtype)


def chunk_limit(pos):
    return (pos // CHUNK + 1) * CHUNK


def indexer_sparse_attention(q, k, v, q_idx, k_idx, w_idx, pos):
    B, S, H, dh = q.shape
    topk = min(TOPK_MAX, S // 4)
    nblk = S // SPARSE_Q_BLOCK
    key_pos = jnp.arange(S, dtype=jnp.int32)
    idx_scale = (IDX_DIM * IDX_HEADS) ** -0.5
    att_scale = dh ** -0.5

    def to_blocks(a):
        return jnp.moveaxis(a.reshape(B, nblk, SPARSE_Q_BLOCK, *a.shape[2:]), 1, 0)

    def block(args):
        qb, qib, wb, pb = args
        limit = chunk_limit(pb)
        rel = jax.nn.relu(jnp.einsum('bqhd,bsd->bqhs', qib, k_idx).astype(jnp.float32))
        score = jnp.einsum('bqhs,bqh->bqs', rel, wb.astype(jnp.float32)) * idx_scale
        admissible = key_pos[None, :] < limit[:, None]
        score = jnp.where(admissible[None], score, NEG_INF)
        _, sel = lax.top_k(score, topk)
        valid = sel < limit[None, :, None]
        k_sel = jax.vmap(lambda kb, ib: kb[ib])(k, sel)
        v_sel = jax.vmap(lambda vb, ib: vb[ib])(v, sel)
        logits = jnp.einsum('bqhd,bqkhd->bhqk', qb, k_sel).astype(jnp.float32) * att_scale
        logits = jnp.where(valid[:, None], logits, NEG_INF)
        p = jax.nn.softmax(logits, axis=-1).astype(v.dtype)
        return jnp.einsum('bhqk,bqkhd->bqhd', p, v_sel)

    out = lax.map(block, (to_blocks(q), to_blocks(q_idx), to_blocks(w_idx),
                          pos.reshape(nblk, SPARSE_Q_BLOCK)))
    return jnp.moveaxis(out, 0, 1).reshape(B, S, H, dh)


def chunk_causal_attention(q, k, v, pos):
    B, S, H, dq = q.shape
    dv = v.shape[-1]
    nblk = S // DENSE_Q_BLOCK
    scale = dq ** -0.5

    def block(args):
        qb, pb = args
        logits = jnp.einsum('bqhd,bshd->bhqs', qb, k).astype(jnp.float32) * scale
        mask = pos[None, :] < chunk_limit(pb)[:, None]
        logits = jnp.where(mask[None, None], logits, NEG_INF)
        p = jax.nn.softmax(logits, axis=-1).astype(v.dtype)
        return jnp.einsum('bhqs,bshd->bqhd', p, v)

    qbl = jnp.moveaxis(q.reshape(B, nblk, DENSE_Q_BLOCK, H, dq), 1, 0)
    out = lax.map(block, (qbl, pos.reshape(nblk, DENSE_Q_BLOCK)))
    return jnp.moveaxis(out, 0, 1).reshape(B, S, H, dv)


def setup_inputs(seed: int = 0) -> dict:
    key = jax.random.key(seed)
    ks = jax.random.split(key, 12)
    f32 = jnp.float32

    def gain(k, n):
        return 1.0 + 0.02 * jax.random.normal(k, (DEPTH, n), f32)

    x = jax.random.normal(ks[0], (BATCH, SEQ, D_MODEL), f32)
    norm_gain = gain(ks[1], D_MODEL)
    w_in = jax.random.normal(ks[2], (DEPTH, D_MODEL, D_IN), f32) * D_MODEL ** -0.5
    a_q_norm = gain(ks[3], A_HEAD_DIM)
    a_k_norm = gain(ks[4], A_HEAD_DIM)
    b_q_latent_norm = gain(ks[5], Q_LORA)
    b_kv_latent_norm = gain(ks[6], KV_LORA)
    w_uq = jax.random.normal(ks[7], (DEPTH, Q_LORA, B_HEADS * B_QK_DIM), f32) * Q_LORA ** -0.5
    w_ukv = jax.random.normal(ks[8], (DEPTH, KV_LORA, B_HEADS * (B_NOPE_DIM + B_V_DIM)), f32) * KV_LORA ** -0.5
    b_q_norm = gain(ks[9], B_QK_DIM)
    b_k_norm = gain(ks[10], B_QK_DIM)
    w_out = jax.random.normal(ks[11], (DEPTH, D_MIX, D_MODEL), f32) * D_MIX ** -0.5
    return {"x": x, "norm_gain": norm_gain, "w_in": w_in, "a_q_norm": a_q_norm,
            "a_k_norm": a_k_norm, "b_q_latent_norm": b_q_latent_norm,
            "b_kv_latent_norm": b_kv_latent_norm, "w_uq": w_uq, "w_ukv": w_ukv,
            "b_q_norm": b_q_norm, "b_k_norm": b_k_norm, "w_out": w_out}


def reference(x, norm_gain, w_in, a_q_norm, a_k_norm, b_q_latent_norm, b_kv_latent_norm,
              w_uq, w_ukv, b_q_norm, b_k_norm, w_out):
    B, S, _ = x.shape
    pos = jnp.arange(S, dtype=jnp.int32)
    offsets = [int(o) for o in np.cumsum(IN_SPLITS)[:-1]]
    h = x
    for l in range(DEPTH):
        xn = rms_norm(h, norm_gain[l])
        proj = xn @ w_in[l]
        (q_a, k_a, v_a, g_a, q_i, k_i, w_i,
         c_q, c_kv, k_rope, g_b) = jnp.split(proj, offsets, axis=-1)

        q_a = rope(rms_norm(q_a.reshape(B, S, A_HEADS, A_HEAD_DIM), a_q_norm[l]), pos)
        k_a = rope(rms_norm(k_a.reshape(B, S, A_HEADS, A_HEAD_DIM), a_k_norm[l]), pos)
        v_a = v_a.reshape(B, S, A_HEADS, A_HEAD_DIM)
        q_i = q_i.reshape(B, S, IDX_HEADS, IDX_DIM)
        q_i = jnp.concatenate([rope(q_i[..., :IDX_ROPE_DIM], pos), q_i[..., IDX_ROPE_DIM:]], axis=-1)
        k_i = k_i[:, :, None, :]
        k_i = jnp.concatenate([rope(k_i[..., :IDX_ROPE_DIM], pos), k_i[..., IDX_ROPE_DIM:]], axis=-1)[:, :, 0, :]
        o_a = indexer_sparse_attention(q_a, k_a, v_a, q_i, k_i, w_i, pos)

        q_b = (rms_norm(c_q, b_q_latent_norm[l]) @ w_uq[l]).reshape(B, S, B_HEADS, B_QK_DIM)
        kv = (rms_norm(c_kv, b_kv_latent_norm[l]) @ w_ukv[l]).reshape(B, S, B_HEADS, B_NOPE_DIM + B_V_DIM)
        k_nope, v_b = kv[..., :B_NOPE_DIM], kv[..., B_NOPE_DIM:]
        k_rope_h = jnp.broadcast_to(k_rope[:, :, None, :], (B, S, B_HEADS, B_ROPE_DIM))
        k_b = jnp.concatenate([k_nope, k_rope_h], axis=-1)
        q_b = rms_norm(q_b, b_q_norm[l])
        k_b = rms_norm(k_b, b_k_norm[l])
        q_b = jnp.concatenate([q_b[..., :B_NOPE_DIM], rope(q_b[..., B_NOPE_DIM:], pos)], axis=-1)
        k_b = jnp.concatenate([k_b[..., :B_NOPE_DIM], rope(k_b[..., B_NOPE_DIM:], pos)], axis=-1)
        o_b = chunk_causal_attention(q_b, k_b, v_b, pos)

        mixed = jnp.concatenate([o_a.reshape(B, S, A_WIDTH) * jax.nn.silu(g_a),
                                 o_b.reshape(B, S, B_WIDTH) * jax.nn.silu(g_b)], axis=-1)
        h = h + mixed @ w_out[l]
    return h
```

```python
import functools

import numpy as np
import jax
import jax.numpy as jnp
from jax import lax
from jax.experimental import pallas as pl
from jax.experimental.pallas import tpu as pltpu

F32 = jnp.float32
BF16 = jnp.bfloat16
I32 = jnp.int32

D_MODEL = 1024
CHUNK = 64
ROPE_THETA = 10000.0
RMS_EPS = 1e-6
NEG_INF = -1e30
HEADS = 8
HEAD_DIM = 64
WIDTH = HEADS * HEAD_DIM
IDX_DIM = 64
IDX_ROPE_DIM = 32
TOPK_MAX = 256
B_NOPE = 64
B_ROPE = 32
B_QK = B_NOPE + B_ROPE
B_PAD = 128
Q_LORA = 384
KV_LORA = 256

O_QA, O_KA, O_VA, O_GA = 0, 512, 1024, 1536
O_QI, O_KI, O_WI = 2048, 2560, 2624
O_CQ, O_CKV, O_KR, O_GB, O_END = 2632, 3016, 3272, 3304, 3816

TM = 512
TQ = 256
TK = 256
V7X_VMEM_LIMIT = 56 * 1024 * 1024

INT_MIN = -(2 ** 31)
F32_LOWEST = float(np.finfo(np.float32).min)
_NEG_BITS = int(np.array(NEG_INF, np.float32).view(np.int32))
NEG_KEY = int(np.int32(np.int64(INT_MIN) - np.int64(_NEG_BITS)))


def _rope_rows(a, b, cos, sin):
    return a * cos - b * sin, b * cos + a * sin


def _head_norm_rope_t(xh, gain, cos, sin, n_true, rope_lo, half):
    ssq = jnp.sum(xh * xh, axis=0, keepdims=True)
    y = (xh * lax.rsqrt(ssq * (1.0 / n_true) + RMS_EPS)) * gain
    ra, rb = _rope_rows(y[rope_lo:rope_lo + half], y[rope_lo + half:rope_lo + 2 * half], cos, sin)
    parts = []
    if rope_lo:
        parts.append(y[:rope_lo])
    parts += [ra, rb]
    if rope_lo + 2 * half < y.shape[0]:
        parts.append(y[rope_lo + 2 * half:])
    return jnp.concatenate(parts, axis=0)


def _silu(g):
    return g * (1.0 / (1.0 + jnp.exp(-g)))


def _proj_kernel(x_ref, ng_ref, w_ref, wuq_ref, wuk_ref, wuv_ref,
                 gaq_ref, gak_ref, gql_ref, gkvl_ref, gbq_ref, gbk_ref,
                 c64_ref, s64_ref, c32_ref, s32_ref,
                 qat_ref, ka_ref, vat_ref, gat_ref, qit_ref, ki_ref, wit_ref,
                 qbt_ref, kb_ref, vbt_ref, gbt_ref):
    x = x_ref[0]
    ms = jnp.mean(x * x, axis=-1, keepdims=True)
    xb = ((x * lax.rsqrt(ms + RMS_EPS)) * ng_ref[...]).astype(BF16)

    def proj_t(lo, hi):
        return lax.dot_general(w_ref[lo:hi, :], xb, (((1,), (1,)), ((), ())),
                               preferred_element_type=F32)

    c64, s64 = c64_ref[...], s64_ref[...]
    c32, s32 = c32_ref[...], s32_ref[...]
    nkt = TM // TK

    def store_tiles(ref, val):
        for t in range(nkt):
            ref[0, t] = val[:, t * TK:(t + 1) * TK]

    qa = proj_t(O_QA, O_KA)
    gaq = gaq_ref[...]
    qa = jnp.concatenate(
        [_head_norm_rope_t(qa[h * 64:(h + 1) * 64], gaq, c64, s64, 64, 0, 32) for h in range(HEADS)], axis=0)
    qat_ref[0] = (qa * (HEAD_DIM ** -0.5)).astype(BF16)

    ka = proj_t(O_KA, O_VA)
    gak = gak_ref[...]
    ka = jnp.concatenate(
        [_head_norm_rope_t(ka[h * 64:(h + 1) * 64], gak, c64, s64, 64, 0, 32) for h in range(HEADS)], axis=0)
    ka_ref[0] = ka.T.astype(BF16)

    vg = proj_t(O_VA, O_QI)
    store_tiles(vat_ref, vg[:WIDTH].astype(BF16))
    gat_ref[0] = _silu(vg[WIDTH:]).astype(BF16)

    qi = proj_t(O_QI, O_KI)
    qi_parts = []
    for h in range(HEADS):
        qh = qi[h * 64:(h + 1) * 64]
        ra, rb = _rope_rows(qh[0:16], qh[16:32], c32, s32)
        qi_parts += [ra, rb, qh[32:]]
    qit_ref[0] = jnp.concatenate(qi_parts, axis=0).astype(BF16)

    kw = proj_t(O_KI, O_CQ)
    ra, rb = _rope_rows(kw[0:16], kw[16:32], c32, s32)
    ki = jnp.concatenate([ra, rb, kw[32:64], jnp.zeros((64, TM), F32)], axis=0)
    ki_ref[0] = ki.T.astype(BF16)
    wit_ref[0] = kw[64:72]

    lat = proj_t(O_CQ, O_GB)
    cq = lat[:Q_LORA]
    cq = (cq * lax.rsqrt(jnp.mean(cq * cq, axis=0, keepdims=True) + RMS_EPS)) * gql_ref[...]
    qb = jnp.dot(wuq_ref[...], cq.astype(BF16), preferred_element_type=F32)
    gbq = gbq_ref[...]
    qb = jnp.concatenate(
        [_head_norm_rope_t(qb[h * B_PAD:(h + 1) * B_PAD], gbq, c32, s32, B_QK, B_NOPE, 16)
         for h in range(HEADS)], axis=0)
    qbt_ref[0] = (qb * (B_QK ** -0.5)).astype(BF16)

    ckv = lat[Q_LORA:Q_LORA + KV_LORA]
    ckv = ((ckv * lax.rsqrt(jnp.mean(ckv * ckv, axis=0, keepdims=True) + RMS_EPS)) * gkvl_ref[...]).astype(BF16)
    kr = lat[Q_LORA + KV_LORA:]
    kn = jnp.dot(wuk_ref[...], ckv, preferred_element_type=F32)
    gbk = gbk_ref[...]
    zpad = jnp.zeros((B_PAD - B_QK, TM), F32)
    kb = jnp.concatenate(
        [_head_norm_rope_t(jnp.concatenate([kn[h * 64:(h + 1) * 64], kr, zpad], axis=0),
                           gbk, c32, s32, B_QK, B_NOPE, 16) for h in range(HEADS)], axis=0)
    kb_ref[0] = kb.T.astype(BF16)
    vb = jnp.dot(wuv_ref[...], ckv, preferred_element_type=F32)
    store_tiles(vbt_ref, vb.astype(BF16))

    gbt_ref[0] = _silu(proj_t(O_GB, O_END)).astype(BF16)


def _softmax_step(h, s, vt, m_sc, l_sc, acc_sc):
    m_old = m_sc[h]
    m_new = jnp.maximum(m_old, jnp.max(s, axis=0, keepdims=True))
    alpha = jnp.exp(m_old - m_new)
    p = jnp.exp(s - m_new)
    l_sc[h] = alpha * l_sc[h] + jnp.sum(p, axis=0, keepdims=True)
    acc_sc[h] = alpha * acc_sc[h] + jnp.dot(vt, p.astype(BF16), preferred_element_type=F32)
    m_sc[h] = m_new


def _init_stats(m_sc, l_sc, acc_sc):
    m_sc[...] = jnp.full(m_sc.shape, F32_LOWEST, F32)
    l_sc[...] = jnp.zeros(l_sc.shape, F32)
    acc_sc[...] = jnp.zeros(acc_sc.shape, F32)


def _finish(out_ref, gate_ref, m_sc, l_sc, acc_sc):
    for h in range(HEADS):
        o = acc_sc[h] * (1.0 / l_sc[h])
        g = gate_ref[0, h * 64:(h + 1) * 64, :].astype(F32)
        out_ref[0, h * 64:(h + 1) * 64, :] = (o * g).astype(BF16)


def _chunk_admissible():
    rows = lax.broadcasted_iota(I32, (TK, TQ), 0)
    cols = lax.broadcasted_iota(I32, (TK, TQ), 1)
    return (rows >> 6) <= (cols >> 6)


def _dsa_kernel(seq_len, ki_ref, ka_ref, vat_ref, qit_ref, wit_ref, qat_ref, gat_ref, out_ref,
                key_sc, bias_sc, m_sc, l_sc, acc_sc):
    j = pl.program_id(1)
    nk = j + 1
    n_extra = seq_len - nk * TK
    topk = min(TOPK_MAX, seq_len // 4)
    idx_scale = (IDX_DIM * HEADS) ** -0.5
    adm_diag = _chunk_admissible()
    zeros64 = jnp.zeros((64, TQ), BF16)

    qi = qit_ref[0]
    w = wit_ref[0]
    qpads = [jnp.concatenate([qi[h * 64:(h + 1) * 64], zeros64], axis=0) for h in range(HEADS)]

    def score_body(kt, carry):
        r0 = pl.multiple_of(kt * TK, TK)
        kt_tile = ki_ref[0, pl.ds(r0, TK), :]
        acc = jnp.zeros((TK, TQ), F32)
        for h in range(HEADS):
            rel = jnp.dot(kt_tile, qpads[h], preferred_element_type=F32)
            acc = acc + jnp.maximum(rel, 0.0) * w[h:h + 1, :]
        score = acc * idx_scale
        adm = jnp.logical_or(adm_diag, kt < j)
        score = jnp.where(adm, score, NEG_INF)
        bits = lax.bitcast_convert_type(score, I32)
        key_sc[pl.ds(r0, TK), :] = jnp.where(bits < 0, INT_MIN - bits, bits)
        return carry

    lax.fori_loop(0, nk, score_body, 0)

    def count(pred):
        def body(kt, c):
            r0 = pl.multiple_of(kt * TK, TK)
            hit = pred(key_sc[pl.ds(r0, TK), :], kt).astype(I32)
            return c + jnp.sum(hit.reshape(TK // 8, 8, TQ), axis=0)
        part = lax.fori_loop(0, nk, body, jnp.zeros((8, TQ), I32))
        return jnp.sum(part, axis=0, keepdims=True)

    def count_ge(cand):
        extra = jnp.where(cand <= NEG_KEY, n_extra, 0)
        return count(lambda k, kt: k >= cand) + extra

    def bit_body(i, t_u):
        c_u = t_u | jnp.left_shift(jnp.int32(1), 31 - i)
        cnt = count_ge(c_u ^ INT_MIN)
        return jnp.where(cnt >= topk, c_u, t_u)

    thr = lax.fori_loop(0, 32, bit_body, jnp.zeros((1, TQ), I32)) ^ INT_MIN
    cnt_ge = count_ge(thr)
    has_tie = jnp.max(jnp.where(cnt_ge != topk, 1, 0)) > 0

    row_iota = lax.broadcasted_iota(I32, (TK, TQ), 0)

    def tie_limit():
        cnt_gt = count(lambda k, kt: k > thr) + jnp.where(thr < NEG_KEY, n_extra, 0)
        need = topk - cnt_gt

        def idx_body(i, p):
            c = p | jnp.left_shift(jnp.int32(1), 10 - i)
            below = count(lambda k, kt: jnp.logical_and(k == thr, row_iota + kt * TK < c))
            return jnp.where(below < need, c, p)

        return lax.fori_loop(0, 11, idx_body, jnp.zeros((1, TQ), I32))

    p_lim = lax.cond(has_tie, tie_limit, lambda: jnp.full((1, TQ), seq_len, I32))

    def bias_body(kt, carry):
        r0 = pl.multiple_of(kt * TK, TK)
        k = key_sc[pl.ds(r0, TK), :]
        sel = jnp.logical_or(k > thr, jnp.logical_and(k == thr, row_iota + kt * TK <= p_lim))
        sel = jnp.logical_and(sel, jnp.logical_or(adm_diag, kt < j))
        bias_sc[pl.ds(r0, TK), :] = jnp.where(sel, 0.0, -jnp.inf)
        return carry

    lax.fori_loop(0, nk, bias_body, 0)

    _init_stats(m_sc, l_sc, acc_sc)
    qa = qat_ref[0]
    qa_pads = []
    for h in range(HEADS):
        qh = qa[h * 64:(h + 1) * 64]
        qa_pads.append(jnp.concatenate([zeros64, qh] if h % 2 else [qh, zeros64], axis=0))

    def att_body(kt, carry):
        r0 = pl.multiple_of(kt * TK, TK)
        bias = bias_sc[pl.ds(r0, TK), :]
        for h in range(HEADS):
            kpair = ka_ref[0, pl.ds(r0, TK), (h // 2) * 128:(h // 2 + 1) * 128]
            s = jnp.dot(kpair, qa_pads[h], preferred_element_type=F32) + bias
            _softmax_step(h, s, vat_ref[0, kt, h * 64:(h + 1) * 64, :], m_sc, l_sc, acc_sc)
        return carry

    lax.fori_loop(0, nk, att_body, 0)
    _finish(out_ref, gat_ref, m_sc, l_sc, acc_sc)


def _mla_kernel(kb_ref, vbt_ref, qbt_ref, gbt_ref, out_ref, m_sc, l_sc, acc_sc):
    j = pl.program_id(1)
    _init_stats(m_sc, l_sc, acc_sc)
    qb = qbt_ref[0]

    def tile(kt, bias):
        r0 = pl.multiple_of(kt * TK, TK)
        for h in range(HEADS):
            kh = kb_ref[0, pl.ds(r0, TK), h * B_PAD:(h + 1) * B_PAD]
            s = jnp.dot(kh, qb[h * B_PAD:(h + 1) * B_PAD], preferred_element_type=F32)
            if bias is not None:
                s = s + bias
            _softmax_step(h, s, vbt_ref[0, kt, h * 64:(h + 1) * 64, :], m_sc, l_sc, acc_sc)

    def body(kt, carry):
        tile(kt, None)
        return carry

    lax.fori_loop(0, j, body, 0)
    tile(j, jnp.where(_chunk_admissible(), 0.0, -jnp.inf))
    _finish(out_ref, gbt_ref, m_sc, l_sc, acc_sc)


def _out_kernel(x_ref, mat_ref, mbt_ref, w_ref, out_ref):
    mixed_t = jnp.concatenate([mat_ref[0], mbt_ref[0]], axis=0)
    y = lax.dot_general(mixed_t, w_ref[...], (((0,), (0,)), ((), ())),
                        preferred_element_type=F32)
    out_ref[0] = x_ref[0] + y


def _rope_tables_t(seq_len, dim):
    half = dim // 2
    freqs = jnp.power(ROPE_THETA, -jnp.arange(half, dtype=F32) * 2.0 / dim)
    ang = jnp.arange(seq_len, dtype=jnp.int32).astype(F32)[None, :] * freqs[:, None]
    return jnp.cos(ang), jnp.sin(ang)


def _params(sem):
    return pltpu.CompilerParams(dimension_semantics=sem, vmem_limit_bytes=V7X_VMEM_LIMIT)


def _layer(h, norm_gain, w_in, a_q_norm, a_k_norm, b_q_latent_norm, b_kv_latent_norm,
           w_uq, w_ukv, b_q_norm, b_k_norm, w_out):
    B, S, D = h.shape
    assert D == D_MODEL and S % TM == 0 and TM % TK == 0 and TQ == TK and S <= 2048
    nst, nkt, nqb = S // TM, S // TK, S // TQ

    w_t = w_in.T.astype(BF16)
    wuq_t = jnp.pad(w_uq.reshape(Q_LORA, HEADS, B_QK), ((0, 0), (0, 0), (0, B_PAD - B_QK)))
    wuq_t = wuq_t.reshape(Q_LORA, HEADS * B_PAD).T.astype(BF16)
    wukv = w_ukv.reshape(KV_LORA, HEADS, B_NOPE + HEAD_DIM)
    wuk_t = wukv[:, :, :B_NOPE].reshape(KV_LORA, WIDTH).T.astype(BF16)
    wuv_t = wukv[:, :, B_NOPE:].reshape(KV_LORA, WIDTH).T.astype(BF16)
    lanes = lambda g: jnp.broadcast_to(g.astype(F32)[:, None], (g.shape[0], TM))
    pad_b = lambda g: jnp.pad(g, (0, B_PAD - B_QK))
    c64, s64 = _rope_tables_t(S, HEAD_DIM)
    c32, s32 = _rope_tables_t(S, B_ROPE)

    const = lambda shape: pl.BlockSpec(shape, lambda b, i: (0,) * len(shape))
    tok_t = lambda c: pl.BlockSpec((1, c, TM), lambda b, i: (b, 0, i))
    tok = lambda c: pl.BlockSpec((1, TM, c), lambda b, i: (b, i, 0))
    tiles = lambda c: pl.BlockSpec((1, TM // TK, c, TK), lambda b, i: (b, i, 0, 0))
    tab = lambda r: pl.BlockSpec((r, TM), lambda b, i: (0, i))
    sds = jax.ShapeDtypeStruct

    (qat, ka, vat, gat, qit, ki, wit, qbt, kb, vbt, gbt) = pl.pallas_call(
        _proj_kernel,
        grid=(B, nst),
        in_specs=[tok(D), const((1, D)), const((O_END, D)), const((HEADS * B_PAD, Q_LORA)),
                  const((WIDTH, KV_LORA)), const((WIDTH, KV_LORA)),
                  const((64, TM)), const((64, TM)), const((Q_LORA, TM)), const((KV_LORA, TM)),
                  const((B_PAD, TM)), const((B_PAD, TM)),
                  tab(32), tab(32), tab(16), tab(16)],
        out_specs=[tok_t(WIDTH), tok(WIDTH), tiles(WIDTH), tok_t(WIDTH), tok_t(WIDTH), tok(128),
                   tok_t(HEADS), tok_t(HEADS * B_PAD), tok(HEADS * B_PAD), tiles(WIDTH), tok_t(WIDTH)],
        out_shape=[sds((B, WIDTH, S), BF16), sds((B, S, WIDTH), BF16), sds((B, nkt, WIDTH, TK), BF16),
                   sds((B, WIDTH, S), BF16), sds((B, WIDTH, S), BF16), sds((B, S, 128), BF16),
                   sds((B, HEADS, S), F32), sds((B, HEADS * B_PAD, S), BF16),
                   sds((B, S, HEADS * B_PAD), BF16), sds((B, nkt, WIDTH, TK), BF16),
                   sds((B, WIDTH, S), BF16)],
        compiler_params=_params(("arbitrary", "arbitrary")),
        name="proj",
    )(h, norm_gain.reshape(1, D), w_t, wuq_t, wuk_t, wuv_t,
      lanes(a_q_norm), lanes(a_k_norm), lanes(b_q_latent_norm), lanes(b_kv_latent_norm),
      lanes(pad_b(b_q_norm)), lanes(pad_b(b_k_norm)), c64, s64, c32, s32)

    per_b = lambda *shape: pl.BlockSpec((1,) + shape, lambda b, j: (b,) + (0,) * len(shape))
    qblk = lambda c: pl.BlockSpec((1, c, TQ), lambda b, j: (b, 0, j))
    stats = [pltpu.VMEM((HEADS, 1, TQ), F32), pltpu.VMEM((HEADS, 1, TQ), F32),
             pltpu.VMEM((HEADS, 64, TQ), F32)]

    mixed_a = pl.pallas_call(
        functools.partial(_dsa_kernel, S),
        grid=(B, nqb),
        in_specs=[per_b(S, 128), per_b(S, WIDTH), per_b(nkt, WIDTH, TK),
                  qblk(WIDTH), qblk(HEADS), qblk(WIDTH), qblk(WIDTH)],
        out_specs=qblk(WIDTH),
        out_shape=sds((B, WIDTH, S), BF16),
        scratch_shapes=[pltpu.VMEM((S, TQ), I32), pltpu.VMEM((S, TQ), F32)] + stats,
        compiler_params=_params(("arbitrary", "arbitrary")),
        name="dsa",
    )(ki, ka, vat, qit, wit, qat, gat)

    mixed_b = pl.pallas_call(
        _mla_kernel,
        grid=(B, nqb),
        in_specs=[per_b(S, HEADS * B_PAD), per_b(nkt, WIDTH, TK), qblk(HEADS * B_PAD), qblk(WIDTH)],
        out_specs=qblk(WIDTH),
        out_shape=sds((B, WIDTH, S), BF16),
        scratch_shapes=stats,
        compiler_params=_params(("arbitrary", "arbitrary")),
        name="mla",
    )(kb, vbt, qbt, gbt)

    return pl.pallas_call(
        _out_kernel,
        grid=(B, nst),
        in_specs=[tok(D), tok_t(WIDTH), tok_t(WIDTH), const((2 * WIDTH, D))],
        out_specs=tok(D),
        out_shape=sds((B, S, D), F32),
        compiler_params=_params(("arbitrary", "arbitrary")),
        name="outproj",
    )(h, mixed_a, mixed_b, w_out.astype(BF16))


def kernel(x, norm_gain, w_in, a_q_norm, a_k_norm, b_q_latent_norm, b_kv_latent_norm,
           w_uq, w_ukv, b_q_norm, b_k_norm, w_out):
    h = x
    for l in range(norm_gain.shape[0]):
        h = _layer(h, norm_gain[l], w_in[l], a_q_norm[l], a_k_norm[l], b_q_latent_norm[l],
                   b_kv_latent_norm[l], w_uq[l], w_ukv[l], b_q_norm[l], b_k_norm[l], w_out[l])
    return h
```

```python
import functools

import numpy as np
import jax
import jax.numpy as jnp
from jax import lax
from jax.experimental import pallas as pl
from jax.experimental.pallas import tpu as pltpu

F32 = jnp.float32
BF16 = jnp.bfloat16
I32 = jnp.int32

D_MODEL = 1024
CHUNK = 64
ROPE_THETA = 10000.0
RMS_EPS = 1e-6
NEG_INF = -1e30
HEADS = 8
HEAD_DIM = 64
WIDTH = HEADS * HEAD_DIM
IDX_DIM = 64
IDX_ROPE_DIM = 32
TOPK_MAX = 256
B_NOPE = 64
B_ROPE = 32
B_QK = B_NOPE + B_ROPE
B_PAD = 128
Q_LORA = 384
KV_LORA = 256
V_EXT = 80

O_QA, O_KA, O_VA, O_GA = 0, 512, 1024, 1536
O_QI, O_KI, O_WI = 2048, 2560, 2624
O_CQ, O_CKV, O_KR, O_GB, O_END = 2632, 3016, 3272, 3304, 3816

TM = 512
TQ = 256
TK = 256
V7X_VMEM_LIMIT = 56 * 1024 * 1024

INT_MIN = -(2 ** 31)
F32_LOWEST = float(np.finfo(np.float32).min)
_NEG_BITS = int(np.array(NEG_INF, np.float32).view(np.int32))
NEG_KEY = int(np.int32(np.int64(INT_MIN) - np.int64(_NEG_BITS)))


def _rope_rows(a, b, cos, sin):
    return a * cos - b * sin, b * cos + a * sin


def _head_norm_rope_t(xh, gain, cos, sin, n_true, rope_lo, half):
    ssq = jnp.sum(xh * xh, axis=0, keepdims=True)
    y = (xh * lax.rsqrt(ssq * (1.0 / n_true) + RMS_EPS)) * gain
    ra, rb = _rope_rows(y[rope_lo:rope_lo + half], y[rope_lo + half:rope_lo + 2 * half], cos, sin)
    parts = []
    if rope_lo:
        parts.append(y[:rope_lo])
    parts += [ra, rb]
    if rope_lo + 2 * half < y.shape[0]:
        parts.append(y[rope_lo + 2 * half:])
    return jnp.concatenate(parts, axis=0)


def _silu(g):
    return g * (1.0 / (1.0 + jnp.exp(-g)))


def _proj_kernel(x_ref, ng_ref, w_ref, wuq_ref, wuk_ref, wuv_ref,
                 gaq_ref, gak_ref, gql_ref, gkvl_ref, gbq_ref, gbk_ref,
                 c64_ref, s64_ref, c32_ref, s32_ref,
                 qat_ref, ka_ref, vat_ref, gat_ref, qit_ref, ki_ref, wit_ref,
                 qbt_ref, kb_ref, vbt_ref, gbt_ref):
    x = x_ref[0]
    ms = jnp.mean(x * x, axis=-1, keepdims=True)
    xb = ((x * lax.rsqrt(ms + RMS_EPS)) * ng_ref[...]).astype(BF16)

    def proj_t(lo, hi):
        return lax.dot_general(w_ref[lo:hi, :], xb, (((1,), (1,)), ((), ())),
                               preferred_element_type=F32)

    c64, s64 = c64_ref[...], s64_ref[...]
    c32, s32 = c32_ref[...], s32_ref[...]
    nkt = TM // TK
    ones_rows = jnp.ones((V_EXT - HEAD_DIM, TM), F32)

    def store_tiles(ref, val):
        ext = jnp.concatenate(
            [r for h in range(HEADS) for r in (val[h * 64:(h + 1) * 64], ones_rows)], axis=0).astype(BF16)
        for t in range(nkt):
            ref[0, t] = ext[:, t * TK:(t + 1) * TK]

    qa = proj_t(O_QA, O_KA)
    gaq = gaq_ref[...]
    qa = jnp.concatenate(
        [_head_norm_rope_t(qa[h * 64:(h + 1) * 64], gaq, c64, s64, 64, 0, 32) for h in range(HEADS)], axis=0)
    qat_ref[0] = (qa * (HEAD_DIM ** -0.5)).astype(BF16)

    ka = proj_t(O_KA, O_VA)
    gak = gak_ref[...]
    ka = jnp.concatenate(
        [_head_norm_rope_t(ka[h * 64:(h + 1) * 64], gak, c64, s64, 64, 0, 32) for h in range(HEADS)], axis=0)
    ka_ref[0] = ka.T.astype(BF16)

    vg = proj_t(O_VA, O_QI)
    store_tiles(vat_ref, vg[:WIDTH])
    gat_ref[0] = _silu(vg[WIDTH:]).astype(BF16)

    qi = proj_t(O_QI, O_KI)
    qi_parts = []
    for h in range(HEADS):
        qh = qi[h * 64:(h + 1) * 64]
        ra, rb = _rope_rows(qh[0:16], qh[16:32], c32, s32)
        qi_parts += [ra, rb, qh[32:]]
    qit_ref[0] = jnp.concatenate(qi_parts, axis=0).astype(BF16)

    kw = proj_t(O_KI, O_CQ)
    ra, rb = _rope_rows(kw[0:16], kw[16:32], c32, s32)
    ki = jnp.concatenate([ra, rb, kw[32:64], jnp.zeros((64, TM), F32)], axis=0)
    ki_ref[0] = ki.T.astype(BF16)
    wit_ref[0] = kw[64:72]

    lat = proj_t(O_CQ, O_GB)
    cq = lat[:Q_LORA]
    cq = (cq * lax.rsqrt(jnp.mean(cq * cq, axis=0, keepdims=True) + RMS_EPS)) * gql_ref[...]
    qb = jnp.dot(wuq_ref[...], cq.astype(BF16), preferred_element_type=F32)
    gbq = gbq_ref[...]
    qb = jnp.concatenate(
        [_head_norm_rope_t(qb[h * B_PAD:(h + 1) * B_PAD], gbq, c32, s32, B_QK, B_NOPE, 16)
         for h in range(HEADS)], axis=0)
    qbt_ref[0] = (qb * (B_QK ** -0.5)).astype(BF16)

    ckv = lat[Q_LORA:Q_LORA + KV_LORA]
    ckv = ((ckv * lax.rsqrt(jnp.mean(ckv * ckv, axis=0, keepdims=True) + RMS_EPS)) * gkvl_ref[...]).astype(BF16)
    kr = lat[Q_LORA + KV_LORA:]
    kn = jnp.dot(wuk_ref[...], ckv, preferred_element_type=F32)
    gbk = gbk_ref[...]
    zpad = jnp.zeros((B_PAD - B_QK, TM), F32)
    kb = jnp.concatenate(
        [_head_norm_rope_t(jnp.concatenate([kn[h * 64:(h + 1) * 64], kr, zpad], axis=0),
                           gbk, c32, s32, B_QK, B_NOPE, 16) for h in range(HEADS)], axis=0)
    kb_ref[0] = kb.T.astype(BF16)
    vb = jnp.dot(wuv_ref[...], ckv, preferred_element_type=F32)
    store_tiles(vbt_ref, vb)

    gbt_ref[0] = _silu(proj_t(O_GB, O_END)).astype(BF16)


def _tile_rows(kt):
    return pl.ds(pl.multiple_of(kt * TK, TK), TK)


def _store_logits(h, kt, s, mrun, s_sc):
    s_sc[h, _tile_rows(kt), :] = s
    return jnp.maximum(mrun, jnp.max(s.reshape(TK // 8, 8, TQ), axis=0))


def _exp_pv(nk, mrun, vt_ref, gate_ref, out_ref, s_sc, acc_sc):
    m = [jnp.max(mrun[h], axis=0, keepdims=True) for h in range(HEADS)]
    acc_sc[...] = jnp.zeros(acc_sc.shape, F32)

    def body(kt, carry):
        for h in range(HEADS):
            p = jnp.exp(s_sc[h, _tile_rows(kt), :] - m[h]).astype(BF16)
            acc_sc[h] += jnp.dot(vt_ref[0, kt, h * V_EXT:(h + 1) * V_EXT, :], p,
                                 preferred_element_type=F32)
        return carry

    lax.fori_loop(0, nk, body, 0)
    for h in range(HEADS):
        o = acc_sc[h, :HEAD_DIM] * (1.0 / acc_sc[h, HEAD_DIM:HEAD_DIM + 1])
        g = gate_ref[0, h * 64:(h + 1) * 64, :].astype(F32)
        out_ref[0, h * 64:(h + 1) * 64, :] = (o * g).astype(BF16)


def _init_max():
    return tuple(jnp.full((8, TQ), F32_LOWEST, F32) for _ in range(HEADS))


def _chunk_admissible():
    rows = lax.broadcasted_iota(I32, (TK, TQ), 0)
    cols = lax.broadcasted_iota(I32, (TK, TQ), 1)
    return (rows >> 6) <= (cols >> 6)


def _dsa_kernel(seq_len, ki_ref, ka_ref, vat_ref, qit_ref, wit_ref, qat_ref, gat_ref, out_ref,
                key_sc, s_sc, acc_sc):
    j = pl.program_id(1)
    nk = j + 1
    n_extra = seq_len - nk * TK
    topk = min(TOPK_MAX, seq_len // 4)
    idx_scale = (IDX_DIM * HEADS) ** -0.5
    adm_diag = _chunk_admissible()
    zeros64 = jnp.zeros((64, TQ), BF16)

    qi = qit_ref[0]
    w = wit_ref[0]
    qpads = [jnp.concatenate([qi[h * 64:(h + 1) * 64], zeros64], axis=0) for h in range(HEADS)]

    def score_body(kt, carry):
        kt_tile = ki_ref[0, _tile_rows(kt), :]
        acc = jnp.zeros((TK, TQ), F32)
        for h in range(HEADS):
            rel = jnp.dot(kt_tile, qpads[h], preferred_element_type=F32)
            acc = acc + jnp.maximum(rel, 0.0) * w[h:h + 1, :]
        score = acc * idx_scale
        adm = jnp.logical_or(adm_diag, kt < j)
        score = jnp.where(adm, score, NEG_INF)
        bits = lax.bitcast_convert_type(score, I32)
        key_sc[_tile_rows(kt), :] = jnp.where(bits < 0, INT_MIN - bits, bits)
        return carry

    lax.fori_loop(0, nk, score_body, 0)

    def count(pred):
        def body(kt, c):
            hit = pred(key_sc[_tile_rows(kt), :], kt).astype(I32)
            return c + jnp.sum(hit.reshape(TK // 8, 8, TQ), axis=0)
        part = lax.fori_loop(0, nk, body, jnp.zeros((8, TQ), I32))
        return jnp.sum(part, axis=0, keepdims=True)

    def count_ge(cand):
        extra = jnp.where(cand <= NEG_KEY, n_extra, 0)
        return count(lambda k, kt: k >= cand) + extra

    def bit_body(i, t_u):
        c_u = t_u | jnp.left_shift(jnp.int32(1), 31 - i)
        cnt = count_ge(c_u ^ INT_MIN)
        return jnp.where(cnt >= topk, c_u, t_u)

    thr = lax.fori_loop(0, 32, bit_body, jnp.zeros((1, TQ), I32)) ^ INT_MIN
    cnt_ge = count_ge(thr)
    has_tie = jnp.max(jnp.where(cnt_ge != topk, 1, 0)) > 0

    row_iota = lax.broadcasted_iota(I32, (TK, TQ), 0)

    def tie_limit():
        cnt_gt = count(lambda k, kt: k > thr) + jnp.where(thr < NEG_KEY, n_extra, 0)
        need = topk - cnt_gt

        def idx_body(i, p):
            c = p | jnp.left_shift(jnp.int32(1), 10 - i)
            below = count(lambda k, kt: jnp.logical_and(k == thr, row_iota + kt * TK < c))
            return jnp.where(below < need, c, p)

        return lax.fori_loop(0, 11, idx_body, jnp.zeros((1, TQ), I32))

    p_lim = lax.cond(has_tie, tie_limit, lambda: jnp.full((1, TQ), seq_len, I32))

    qa = qat_ref[0]
    qa_pads = []
    for h in range(HEADS):
        qh = qa[h * 64:(h + 1) * 64]
        qa_pads.append(jnp.concatenate([zeros64, qh] if h % 2 else [qh, zeros64], axis=0))

    def logits_body(kt, mrun):
        k = key_sc[_tile_rows(kt), :]
        sel = jnp.logical_or(k > thr, jnp.logical_and(k == thr, row_iota + kt * TK <= p_lim))
        sel = jnp.logical_and(sel, jnp.logical_or(adm_diag, kt < j))
        bias = jnp.where(sel, 0.0, -jnp.inf)
        new = []
        for h in range(HEADS):
            kpair = ka_ref[0, _tile_rows(kt), (h // 2) * 128:(h // 2 + 1) * 128]
            s = jnp.dot(kpair, qa_pads[h], preferred_element_type=F32) + bias
            new.append(_store_logits(h, kt, s, mrun[h], s_sc))
        return tuple(new)

    mrun = lax.fori_loop(0, nk, logits_body, _init_max())
    _exp_pv(nk, mrun, vat_ref, gat_ref, out_ref, s_sc, acc_sc)


def _mla_kernel(kb_ref, vbt_ref, qbt_ref, gbt_ref, out_ref, s_sc, acc_sc):
    j = pl.program_id(1)
    qb = qbt_ref[0]

    def tile(kt, mrun, bias):
        new = []
        for h in range(HEADS):
            kh = kb_ref[0, _tile_rows(kt), h * B_PAD:(h + 1) * B_PAD]
            s = jnp.dot(kh, qb[h * B_PAD:(h + 1) * B_PAD], preferred_element_type=F32)
            if bias is not None:
                s = s + bias
            new.append(_store_logits(h, kt, s, mrun[h], s_sc))
        return tuple(new)

    mrun = lax.fori_loop(0, j, lambda kt, mrun: tile(kt, mrun, None), _init_max())
    mrun = tile(j, mrun, jnp.where(_chunk_admissible(), 0.0, -jnp.inf))
    _exp_pv(j + 1, mrun, vbt_ref, gbt_ref, out_ref, s_sc, acc_sc)


def _out_kernel(x_ref, mat_ref, mbt_ref, w_ref, out_ref):
    mixed_t = jnp.concatenate([mat_ref[0], mbt_ref[0]], axis=0)
    y = lax.dot_general(mixed_t, w_ref[...], (((0,), (0,)), ((), ())),
                        preferred_element_type=F32)
    out_ref[0] = x_ref[0] + y


def _rope_tables_t(seq_len, dim):
    half = dim // 2
    freqs = jnp.power(ROPE_THETA, -jnp.arange(half, dtype=F32) * 2.0 / dim)
    ang = jnp.arange(seq_len, dtype=jnp.int32).astype(F32)[None, :] * freqs[:, None]
    return jnp.cos(ang), jnp.sin(ang)


def _params(sem):
    return pltpu.CompilerParams(dimension_semantics=sem, vmem_limit_bytes=V7X_VMEM_LIMIT)


def _layer(h, norm_gain, w_in, a_q_norm, a_k_norm, b_q_latent_norm, b_kv_latent_norm,
           w_uq, w_ukv, b_q_norm, b_k_norm, w_out):
    B, S, D = h.shape
    assert D == D_MODEL and S % TM == 0 and TM % TK == 0 and TQ == TK and S <= 2048
    nst, nkt, nqb = S // TM, S // TK, S // TQ
    vext = HEADS * V_EXT

    w_t = w_in.T.astype(BF16)
    wuq_t = jnp.pad(w_uq.reshape(Q_LORA, HEADS, B_QK), ((0, 0), (0, 0), (0, B_PAD - B_QK)))
    wuq_t = wuq_t.reshape(Q_LORA, HEADS * B_PAD).T.astype(BF16)
    wukv = w_ukv.reshape(KV_LORA, HEADS, B_NOPE + HEAD_DIM)
    wuk_t = wukv[:, :, :B_NOPE].reshape(KV_LORA, WIDTH).T.astype(BF16)
    wuv_t = wukv[:, :, B_NOPE:].reshape(KV_LORA, WIDTH).T.astype(BF16)
    lanes = lambda g: jnp.broadcast_to(g.astype(F32)[:, None], (g.shape[0], TM))
    pad_b = lambda g: jnp.pad(g, (0, B_PAD - B_QK))
    c64, s64 = _rope_tables_t(S, HEAD_DIM)
    c32, s32 = _rope_tables_t(S, B_ROPE)

    const = lambda shape: pl.BlockSpec(shape, lambda b, i: (0,) * len(shape))
    tok_t = lambda c: pl.BlockSpec((1, c, TM), lambda b, i: (b, 0, i))
    tok = lambda c: pl.BlockSpec((1, TM, c), lambda b, i: (b, i, 0))
    tiles = lambda c: pl.BlockSpec((1, TM // TK, c, TK), lambda b, i: (b, i, 0, 0))
    tab = lambda r: pl.BlockSpec((r, TM), lambda b, i: (0, i))
    sds = jax.ShapeDtypeStruct

    (qat, ka, vat, gat, qit, ki, wit, qbt, kb, vbt, gbt) = pl.pallas_call(
        _proj_kernel,
        grid=(B, nst),
        in_specs=[tok(D), const((1, D)), const((O_END, D)), const((HEADS * B_PAD, Q_LORA)),
                  const((WIDTH, KV_LORA)), const((WIDTH, KV_LORA)),
                  const((64, TM)), const((64, TM)), const((Q_LORA, TM)), const((KV_LORA, TM)),
                  const((B_PAD, TM)), const((B_PAD, TM)),
                  tab(32), tab(32), tab(16), tab(16)],
        out_specs=[tok_t(WIDTH), tok(WIDTH), tiles(vext), tok_t(WIDTH), tok_t(WIDTH), tok(128),
                   tok_t(HEADS), tok_t(HEADS * B_PAD), tok(HEADS * B_PAD), tiles(vext), tok_t(WIDTH)],
        out_shape=[sds((B, WIDTH, S), BF16), sds((B, S, WIDTH), BF16), sds((B, nkt, vext, TK), BF16),
                   sds((B, WIDTH, S), BF16), sds((B, WIDTH, S), BF16), sds((B, S, 128), BF16),
                   sds((B, HEADS, S), F32), sds((B, HEADS * B_PAD, S), BF16),
                   sds((B, S, HEADS * B_PAD), BF16), sds((B, nkt, vext, TK), BF16),
                   sds((B, WIDTH, S), BF16)],
        compiler_params=_params(("arbitrary", "arbitrary")),
        name="proj",
    )(h, norm_gain.reshape(1, D), w_t, wuq_t, wuk_t, wuv_t,
      lanes(a_q_norm), lanes(a_k_norm), lanes(b_q_latent_norm), lanes(b_kv_latent_norm),
      lanes(pad_b(b_q_norm)), lanes(pad_b(b_k_norm)), c64, s64, c32, s32)

    per_b = lambda *shape: pl.BlockSpec((1,) + shape, lambda b, j: (b,) + (0,) * len(shape))
    qblk = lambda c: pl.BlockSpec((1, c, TQ), lambda b, j: (b, 0, j))
    attn_scratch = [pltpu.VMEM((HEADS, S, TQ), F32), pltpu.VMEM((HEADS, V_EXT, TQ), F32)]

    mixed_a = pl.pallas_call(
        functools.partial(_dsa_kernel, S),
        grid=(B, nqb),
        in_specs=[per_b(S, 128), per_b(S, WIDTH), per_b(nkt, vext, TK),
                  qblk(WIDTH), qblk(HEADS), qblk(WIDTH), qblk(WIDTH)],
        out_specs=qblk(WIDTH),
        out_shape=sds((B, WIDTH, S), BF16),
        scratch_shapes=[pltpu.VMEM((S, TQ), I32)] + attn_scratch,
        compiler_params=_params(("arbitrary", "arbitrary")),
        name="dsa",
    )(ki, ka, vat, qit, wit, qat, gat)

    mixed_b = pl.pallas_call(
        _mla_kernel,
        grid=(B, nqb),
        in_specs=[per_b(S, HEADS * B_PAD), per_b(nkt, vext, TK), qblk(HEADS * B_PAD), qblk(WIDTH)],
        out_specs=qblk(WIDTH),
        out_shape=sds((B, WIDTH, S), BF16),
        scratch_shapes=attn_scratch,
        compiler_params=_params(("arbitrary", "arbitrary")),
        name="mla",
    )(kb, vbt, qbt, gbt)

    return pl.pallas_call(
        _out_kernel,
        grid=(B, nst),
        in_specs=[tok(D), tok_t(WIDTH), tok_t(WIDTH), const((2 * WIDTH, D))],
        out_specs=tok(D),
        out_shape=sds((B, S, D), F32),
        compiler_params=_params(("arbitrary", "arbitrary")),
        name="outproj",
    )(h, mixed_a, mixed_b, w_out.astype(BF16))


def kernel(x, norm_gain, w_in, a_q_norm, a_k_norm, b_q_latent_norm, b_kv_latent_norm,
           w_uq, w_ukv, b_q_norm, b_k_norm, w_out):
    h = x
    for l in range(norm_gain.shape[0]):
        h = _layer(h, norm_gain[l], w_in[l], a_q_norm[l], a_k_norm[l], b_q_latent_norm[l],
                   b_kv_latent_norm[l], w_uq[l], w_ukv[l], b_q_norm[l], b_k_norm[l], w_out[l])
    return h
```

```python
import functools

import numpy as np
import jax
import jax.numpy as jnp
from jax import lax
from jax.experimental import pallas as pl
from jax.experimental.pallas import tpu as pltpu

F32 = jnp.float32
BF16 = jnp.bfloat16
I32 = jnp.int32
I16 = jnp.int16

D_MODEL = 1024
CHUNK = 64
ROPE_THETA = 10000.0
RMS_EPS = 1e-6
NEG_INF = -1e30
HEADS = 8
HEAD_DIM = 64
WIDTH = HEADS * HEAD_DIM
IDX_DIM = 64
IDX_ROPE_DIM = 32
TOPK_MAX = 256
B_NOPE = 64
B_ROPE = 32
B_QK = B_NOPE + B_ROPE
B_PAD = 128
Q_LORA = 384
KV_LORA = 256
V_EXT = 80

O_QA, O_KA, O_VA, O_GA = 0, 512, 1024, 1536
O_QI, O_KI, O_WI = 2048, 2560, 2624
O_CQ, O_CKV, O_KR, O_GB, O_END = 2632, 3016, 3272, 3304, 3816

TM = 512
TQ = 256
TK = 256
V7X_VMEM_LIMIT = 56 * 1024 * 1024

INT_MIN = -(2 ** 31)
F32_LOWEST = float(np.finfo(np.float32).min)
_NEG_BITS = int(np.array(NEG_INF, np.float32).view(np.int32))
NEG_KEY = int(np.int32(np.int64(INT_MIN) - np.int64(_NEG_BITS)))


def _rope_rows(a, b, cos, sin):
    return a * cos - b * sin, b * cos + a * sin


def _head_norm_rope_t(xh, gain, cos, sin, n_true, rope_lo, half):
    ssq = jnp.sum(xh * xh, axis=0, keepdims=True)
    y = (xh * lax.rsqrt(ssq * (1.0 / n_true) + RMS_EPS)) * gain
    ra, rb = _rope_rows(y[rope_lo:rope_lo + half], y[rope_lo + half:rope_lo + 2 * half], cos, sin)
    parts = []
    if rope_lo:
        parts.append(y[:rope_lo])
    parts += [ra, rb]
    if rope_lo + 2 * half < y.shape[0]:
        parts.append(y[rope_lo + 2 * half:])
    return jnp.concatenate(parts, axis=0)


def _silu(g):
    return g * (1.0 / (1.0 + jnp.exp(-g)))


def _proj_kernel(x_ref, ng_ref, w_ref, wuq_ref, wuk_ref, wuv_ref,
                 gaq_ref, gak_ref, gql_ref, gkvl_ref, gbq_ref, gbk_ref,
                 c64_ref, s64_ref, c32_ref, s32_ref,
                 qat_ref, ka_ref, vat_ref, gat_ref, qit_ref, ki_ref, wit_ref,
                 qbt_ref, kb_ref, vbt_ref, gbt_ref):
    x = x_ref[0]
    ms = jnp.mean(x * x, axis=-1, keepdims=True)
    xb = ((x * lax.rsqrt(ms + RMS_EPS)) * ng_ref[...]).astype(BF16)

    def proj_t(lo, hi):
        return lax.dot_general(w_ref[lo:hi, :], xb, (((1,), (1,)), ((), ())),
                               preferred_element_type=F32)

    c64, s64 = c64_ref[...], s64_ref[...]
    c32, s32 = c32_ref[...], s32_ref[...]
    nkt = TM // TK
    ones_rows = jnp.ones((V_EXT - HEAD_DIM, TM), F32)

    def store_tiles(ref, val):
        ext = jnp.concatenate(
            [r for h in range(HEADS) for r in (val[h * 64:(h + 1) * 64], ones_rows)], axis=0).astype(BF16)
        for t in range(nkt):
            ref[0, t] = ext[:, t * TK:(t + 1) * TK]

    qa = proj_t(O_QA, O_KA)
    gaq = gaq_ref[...]
    qa = jnp.concatenate(
        [_head_norm_rope_t(qa[h * 64:(h + 1) * 64], gaq, c64, s64, 64, 0, 32) for h in range(HEADS)], axis=0)
    qat_ref[0] = (qa * (HEAD_DIM ** -0.5)).astype(BF16)

    ka = proj_t(O_KA, O_VA)
    gak = gak_ref[...]
    ka = jnp.concatenate(
        [_head_norm_rope_t(ka[h * 64:(h + 1) * 64], gak, c64, s64, 64, 0, 32) for h in range(HEADS)], axis=0)
    ka_ref[0] = ka.T.astype(BF16)

    vg = proj_t(O_VA, O_QI)
    store_tiles(vat_ref, vg[:WIDTH])
    gat_ref[0] = _silu(vg[WIDTH:]).astype(BF16)

    qi = proj_t(O_QI, O_KI)
    qi_parts = []
    for h in range(HEADS):
        qh = qi[h * 64:(h + 1) * 64]
        ra, rb = _rope_rows(qh[0:16], qh[16:32], c32, s32)
        qi_parts += [ra, rb, qh[32:]]
    qit_ref[0] = jnp.concatenate(qi_parts, axis=0).astype(BF16)

    kw = proj_t(O_KI, O_CQ)
    ra, rb = _rope_rows(kw[0:16], kw[16:32], c32, s32)
    ki = jnp.concatenate([ra, rb, kw[32:64], jnp.zeros((64, TM), F32)], axis=0)
    ki_ref[0] = ki.T.astype(BF16)
    wit_ref[0] = kw[64:72]

    lat = proj_t(O_CQ, O_GB)
    cq = lat[:Q_LORA]
    cq = (cq * lax.rsqrt(jnp.mean(cq * cq, axis=0, keepdims=True) + RMS_EPS)) * gql_ref[...]
    qb = jnp.dot(wuq_ref[...], cq.astype(BF16), preferred_element_type=F32)
    gbq = gbq_ref[...]
    qb = jnp.concatenate(
        [_head_norm_rope_t(qb[h * B_PAD:(h + 1) * B_PAD], gbq, c32, s32, B_QK, B_NOPE, 16)
         for h in range(HEADS)], axis=0)
    qbt_ref[0] = (qb * (B_QK ** -0.5)).astype(BF16)

    ckv = lat[Q_LORA:Q_LORA + KV_LORA]
    ckv = ((ckv * lax.rsqrt(jnp.mean(ckv * ckv, axis=0, keepdims=True) + RMS_EPS)) * gkvl_ref[...]).astype(BF16)
    kr = lat[Q_LORA + KV_LORA:]
    kn = jnp.dot(wuk_ref[...], ckv, preferred_element_type=F32)
    gbk = gbk_ref[...]
    zpad = jnp.zeros((B_PAD - B_QK, TM), F32)
    kb = jnp.concatenate(
        [_head_norm_rope_t(jnp.concatenate([kn[h * 64:(h + 1) * 64], kr, zpad], axis=0),
                           gbk, c32, s32, B_QK, B_NOPE, 16) for h in range(HEADS)], axis=0)
    kb_ref[0] = kb.T.astype(BF16)
    vb = jnp.dot(wuv_ref[...], ckv, preferred_element_type=F32)
    store_tiles(vbt_ref, vb)

    gbt_ref[0] = _silu(proj_t(O_GB, O_END)).astype(BF16)


def _tile_rows(kt):
    return pl.ds(pl.multiple_of(kt * TK, TK), TK)


def _store_logits(h, kt, s, mrun, s_sc):
    s_sc[h, _tile_rows(kt), :] = s
    return jnp.maximum(mrun, jnp.max(s.reshape(TK // 8, 8, TQ), axis=0))


def _exp_pv(nk, mrun, vt_ref, gate_ref, out_ref, s_sc, acc_sc):
    m = [jnp.max(mrun[h], axis=0, keepdims=True) for h in range(HEADS)]
    acc_sc[...] = jnp.zeros(acc_sc.shape, F32)

    def body(kt, carry):
        for h in range(HEADS):
            p = jnp.exp(s_sc[h, _tile_rows(kt), :] - m[h]).astype(BF16)
            acc_sc[h] += jnp.dot(vt_ref[0, kt, h * V_EXT:(h + 1) * V_EXT, :], p,
                                 preferred_element_type=F32)
        return carry

    lax.fori_loop(0, nk, body, 0)
    for h in range(HEADS):
        o = acc_sc[h, :HEAD_DIM] * (1.0 / acc_sc[h, HEAD_DIM:HEAD_DIM + 1])
        g = gate_ref[0, h * 64:(h + 1) * 64, :].astype(F32)
        out_ref[0, h * 64:(h + 1) * 64, :] = (o * g).astype(BF16)


def _init_max():
    return tuple(jnp.full((8, TQ), F32_LOWEST, F32) for _ in range(HEADS))


def _chunk_admissible():
    rows = lax.broadcasted_iota(I32, (TK, TQ), 0)
    cols = lax.broadcasted_iota(I32, (TK, TQ), 1)
    return (rows >> 6) <= (cols >> 6)


def _dsa_kernel(seq_len, ki_ref, ka_ref, vat_ref, qit_ref, wit_ref, qat_ref, gat_ref, out_ref,
                key_sc, hi_sc, lo_sc, s_sc, acc_sc):
    j = pl.program_id(1)
    nk = j + 1
    n_extra = seq_len - nk * TK
    topk = min(TOPK_MAX, seq_len // 4)
    idx_scale = (IDX_DIM * HEADS) ** -0.5
    adm_diag = _chunk_admissible()
    zeros64 = jnp.zeros((64, TQ), BF16)

    qi = qit_ref[0]
    w = wit_ref[0]
    qpads = [jnp.concatenate([qi[h * 64:(h + 1) * 64], zeros64], axis=0) for h in range(HEADS)]

    def score_body(kt, carry):
        kt_tile = ki_ref[0, _tile_rows(kt), :]
        acc = jnp.zeros((TK, TQ), F32)
        for h in range(HEADS):
            rel = jnp.dot(kt_tile, qpads[h], preferred_element_type=F32)
            acc = acc + jnp.maximum(rel, 0.0) * w[h:h + 1, :]
        score = acc * idx_scale
        adm = jnp.logical_or(adm_diag, kt < j)
        score = jnp.where(adm, score, NEG_INF)
        bits = lax.bitcast_convert_type(score, I32)
        key = jnp.where(bits < 0, INT_MIN - bits, bits)
        key_sc[_tile_rows(kt), :] = key
        hi_sc[_tile_rows(kt), :] = (key >> 16).astype(I16)
        lo_sc[_tile_rows(kt), :] = ((key & 0xFFFF) - 32768).astype(I16)
        return carry

    lax.fori_loop(0, nk, score_body, 0)

    def count(pred):
        def body(kt, c):
            hit = pred(key_sc[_tile_rows(kt), :], kt).astype(I32)
            return c + jnp.sum(hit.reshape(TK // 8, 8, TQ), axis=0)
        part = lax.fori_loop(0, nk, body, jnp.zeros((8, TQ), I32))
        return jnp.sum(part, axis=0, keepdims=True)

    def count16(ref, pred):
        def body(kt, c):
            hit = jnp.where(pred(ref[_tile_rows(kt), :]), jnp.int16(1), jnp.int16(0))
            parts = [hit[r * 16:(r + 1) * 16] for r in range(TK // 16)]
            while len(parts) > 1:
                parts = [a + b for a, b in zip(parts[0::2], parts[1::2])]
            return c + parts[0]
        part = lax.fori_loop(0, nk, body, jnp.zeros((16, TQ), I16))
        return jnp.sum(part.astype(I32), axis=0, keepdims=True)

    def kth_largest16(ref, need, extra_ge):
        def bit_body(i, t_u):
            c_u = t_u | jnp.left_shift(jnp.int32(1), 15 - i)
            c_s = c_u - 32768
            c16 = c_s.astype(I16)
            cnt = count16(ref, lambda x: x >= c16) + extra_ge(c_s)
            return jnp.where(cnt >= need, c_u, t_u)
        return lax.fori_loop(0, 16, bit_body, jnp.zeros((1, TQ), I32)) - 32768

    neg_hi, neg_lo = NEG_KEY >> 16, (NEG_KEY & 0xFFFF) - 32768
    t_hi = kth_largest16(hi_sc, topk, lambda c: jnp.where(neg_hi >= c, n_extra, 0))
    t_hi16 = t_hi.astype(I16)
    above = count16(hi_sc, lambda x: x > t_hi16) + jnp.where(neg_hi > t_hi, n_extra, 0)

    def bucket_body(kt, carry):
        rows = _tile_rows(kt)
        lo_sc[rows, :] = jnp.where(hi_sc[rows, :] == t_hi16, lo_sc[rows, :], jnp.int16(-32768))
        return carry

    lax.fori_loop(0, nk, bucket_body, 0)
    t_lo = kth_largest16(
        lo_sc, topk - above,
        lambda c: jnp.where(jnp.logical_and(t_hi == neg_hi, neg_lo >= c), n_extra, 0))
    thr = t_hi * 65536 + (t_lo + 32768)
    cnt_ge = count(lambda k, kt: k >= thr) + jnp.where(thr <= NEG_KEY, n_extra, 0)
    has_tie = jnp.max(jnp.where(cnt_ge != topk, 1, 0)) > 0

    row_iota = lax.broadcasted_iota(I32, (TK, TQ), 0)

    def tie_limit():
        cnt_gt = count(lambda k, kt: k > thr) + jnp.where(thr < NEG_KEY, n_extra, 0)
        need = topk - cnt_gt

        def idx_body(i, p):
            c = p | jnp.left_shift(jnp.int32(1), 10 - i)
            below = count(lambda k, kt: jnp.logical_and(k == thr, row_iota + kt * TK < c))
            return jnp.where(below < need, c, p)

        return lax.fori_loop(0, 11, idx_body, jnp.zeros((1, TQ), I32))

    p_lim = lax.cond(has_tie, tie_limit, lambda: jnp.full((1, TQ), seq_len, I32))

    qa = qat_ref[0]
    qa_pads = []
    for h in range(HEADS):
        qh = qa[h * 64:(h + 1) * 64]
        qa_pads.append(jnp.concatenate([zeros64, qh] if h % 2 else [qh, zeros64], axis=0))

    def logits_body(kt, mrun):
        k = key_sc[_tile_rows(kt), :]
        sel = jnp.logical_or(k > thr, jnp.logical_and(k == thr, row_iota + kt * TK <= p_lim))
        sel = jnp.logical_and(sel, jnp.logical_or(adm_diag, kt < j))
        bias = jnp.where(sel, 0.0, -jnp.inf)
        new = []
        for h in range(HEADS):
            kpair = ka_ref[0, _tile_rows(kt), (h // 2) * 128:(h // 2 + 1) * 128]
            s = jnp.dot(kpair, qa_pads[h], preferred_element_type=F32) + bias
            new.append(_store_logits(h, kt, s, mrun[h], s_sc))
        return tuple(new)

    mrun = lax.fori_loop(0, nk, logits_body, _init_max())
    _exp_pv(nk, mrun, vat_ref, gat_ref, out_ref, s_sc, acc_sc)


def _mla_kernel(kb_ref, vbt_ref, qbt_ref, gbt_ref, out_ref, s_sc, acc_sc):
    j = pl.program_id(1)
    qb = qbt_ref[0]

    def tile(kt, mrun, bias):
        new = []
        for h in range(HEADS):
            kh = kb_ref[0, _tile_rows(kt), h * B_PAD:(h + 1) * B_PAD]
            s = jnp.dot(kh, qb[h * B_PAD:(h + 1) * B_PAD], preferred_element_type=F32)
            if bias is not None:
                s = s + bias
            new.append(_store_logits(h, kt, s, mrun[h], s_sc))
        return tuple(new)

    mrun = lax.fori_loop(0, j, lambda kt, mrun: tile(kt, mrun, None), _init_max())
    mrun = tile(j, mrun, jnp.where(_chunk_admissible(), 0.0, -jnp.inf))
    _exp_pv(j + 1, mrun, vbt_ref, gbt_ref, out_ref, s_sc, acc_sc)


def _out_kernel(x_ref, mat_ref, mbt_ref, w_ref, out_ref):
    mixed_t = jnp.concatenate([mat_ref[0], mbt_ref[0]], axis=0)
    y = lax.dot_general(mixed_t, w_ref[...], (((0,), (0,)), ((), ())),
                        preferred_element_type=F32)
    out_ref[0] = x_ref[0] + y


def _rope_tables_t(seq_len, dim):
    half = dim // 2
    freqs = jnp.power(ROPE_THETA, -jnp.arange(half, dtype=F32) * 2.0 / dim)
    ang = jnp.arange(seq_len, dtype=jnp.int32).astype(F32)[None, :] * freqs[:, None]
    return jnp.cos(ang), jnp.sin(ang)


def _params(sem):
    return pltpu.CompilerParams(dimension_semantics=sem, vmem_limit_bytes=V7X_VMEM_LIMIT)


def _layer(h, norm_gain, w_in, a_q_norm, a_k_norm, b_q_latent_norm, b_kv_latent_norm,
           w_uq, w_ukv, b_q_norm, b_k_norm, w_out):
    B, S, D = h.shape
    assert D == D_MODEL and S % TM == 0 and TM % TK == 0 and TQ == TK and S <= 2048
    nst, nkt, nqb = S // TM, S // TK, S // TQ
    vext = HEADS * V_EXT

    w_t = w_in.T.astype(BF16)
    wuq_t = jnp.pad(w_uq.reshape(Q_LORA, HEADS, B_QK), ((0, 0), (0, 0), (0, B_PAD - B_QK)))
    wuq_t = wuq_t.reshape(Q_LORA, HEADS * B_PAD).T.astype(BF16)
    wukv = w_ukv.reshape(KV_LORA, HEADS, B_NOPE + HEAD_DIM)
    wuk_t = wukv[:, :, :B_NOPE].reshape(KV_LORA, WIDTH).T.astype(BF16)
    wuv_t = wukv[:, :, B_NOPE:].reshape(KV_LORA, WIDTH).T.astype(BF16)
    lanes = lambda g: jnp.broadcast_to(g.astype(F32)[:, None], (g.shape[0], TM))
    pad_b = lambda g: jnp.pad(g, (0, B_PAD - B_QK))
    c64, s64 = _rope_tables_t(S, HEAD_DIM)
    c32, s32 = _rope_tables_t(S, B_ROPE)

    const = lambda shape: pl.BlockSpec(shape, lambda b, i: (0,) * len(shape))
    tok_t = lambda c: pl.BlockSpec((1, c, TM), lambda b, i: (b, 0, i))
    tok = lambda c: pl.BlockSpec((1, TM, c), lambda b, i: (b, i, 0))
    tiles = lambda c: pl.BlockSpec((1, TM // TK, c, TK), lambda b, i: (b, i, 0, 0))
    tab = lambda r: pl.BlockSpec((r, TM), lambda b, i: (0, i))
    sds = jax.ShapeDtypeStruct

    (qat, ka, vat, gat, qit, ki, wit, qbt, kb, vbt, gbt) = pl.pallas_call(
        _proj_kernel,
        grid=(B, nst),
        in_specs=[tok(D), const((1, D)), const((O_END, D)), const((HEADS * B_PAD, Q_LORA)),
                  const((WIDTH, KV_LORA)), const((WIDTH, KV_LORA)),
                  const((64, TM)), const((64, TM)), const((Q_LORA, TM)), const((KV_LORA, TM)),
                  const((B_PAD, TM)), const((B_PAD, TM)),
                  tab(32), tab(32), tab(16), tab(16)],
        out_specs=[tok_t(WIDTH), tok(WIDTH), tiles(vext), tok_t(WIDTH), tok_t(WIDTH), tok(128),
                   tok_t(HEADS), tok_t(HEADS * B_PAD), tok(HEADS * B_PAD), tiles(vext), tok_t(WIDTH)],
        out_shape=[sds((B, WIDTH, S), BF16), sds((B, S, WIDTH), BF16), sds((B, nkt, vext, TK), BF16),
                   sds((B, WIDTH, S), BF16), sds((B, WIDTH, S), BF16), sds((B, S, 128), BF16),
                   sds((B, HEADS, S), F32), sds((B, HEADS * B_PAD, S), BF16),
                   sds((B, S, HEADS * B_PAD), BF16), sds((B, nkt, vext, TK), BF16),
                   sds((B, WIDTH, S), BF16)],
        compiler_params=_params(("arbitrary", "arbitrary")),
        name="proj",
    )(h, norm_gain.reshape(1, D), w_t, wuq_t, wuk_t, wuv_t,
      lanes(a_q_norm), lanes(a_k_norm), lanes(b_q_latent_norm), lanes(b_kv_latent_norm),
      lanes(pad_b(b_q_norm)), lanes(pad_b(b_k_norm)), c64, s64, c32, s32)

    per_b = lambda *shape: pl.BlockSpec((1,) + shape, lambda b, j: (b,) + (0,) * len(shape))
    qblk = lambda c: pl.BlockSpec((1, c, TQ), lambda b, j: (b, 0, j))
    attn_scratch = [pltpu.VMEM((HEADS, S, TQ), F32), pltpu.VMEM((HEADS, V_EXT, TQ), F32)]

    mixed_a = pl.pallas_call(
        functools.partial(_dsa_kernel, S),
        grid=(B, nqb),
        in_specs=[per_b(S, 128), per_b(S, WIDTH), per_b(nkt, vext, TK),
                  qblk(WIDTH), qblk(HEADS), qblk(WIDTH), qblk(WIDTH)],
        out_specs=qblk(WIDTH),
        out_shape=sds((B, WIDTH, S), BF16),
        scratch_shapes=[pltpu.VMEM((S, TQ), I32), pltpu.VMEM((S, TQ), I16),
                        pltpu.VMEM((S, TQ), I16)] + attn_scratch,
        compiler_params=_params(("arbitrary", "arbitrary")),
        name="dsa",
    )(ki, ka, vat, qit, wit, qat, gat)

    mixed_b = pl.pallas_call(
        _mla_kernel,
        grid=(B, nqb),
        in_specs=[per_b(S, HEADS * B_PAD), per_b(nkt, vext, TK), qblk(HEADS * B_PAD), qblk(WIDTH)],
        out_specs=qblk(WIDTH),
        out_shape=sds((B, WIDTH, S), BF16),
        scratch_shapes=attn_scratch,
        compiler_params=_params(("arbitrary", "arbitrary")),
        name="mla",
    )(kb, vbt, qbt, gbt)

    return pl.pallas_call(
        _out_kernel,
        grid=(B, nst),
        in_specs=[tok(D), tok_t(WIDTH), tok_t(WIDTH), const((2 * WIDTH, D))],
        out_specs=tok(D),
        out_shape=sds((B, S, D), F32),
        compiler_params=_params(("arbitrary", "arbitrary")),
        name="outproj",
    )(h, mixed_a, mixed_b, w_out.astype(BF16))


def kernel(x, norm_gain, w_in, a_q_norm, a_k_norm, b_q_latent_norm, b_kv_latent_norm,
           w_uq, w_ukv, b_q_norm, b_k_norm, w_out):
    h = x
    for l in range(norm_gain.shape[0]):
        h = _layer(h, norm_gain[l], w_in[l], a_q_norm[l], a_k_norm[l], b_q_latent_norm[l],
                   b_kv_latent_norm[l], w_uq[l], w_ukv[l], b_q_norm[l], b_k_norm[l], w_out[l])
    return h
```

```python
import functools

import numpy as np
import jax
import jax.numpy as jnp
from jax import lax
from jax.experimental import pallas as pl
from jax.experimental.pallas import tpu as pltpu

F32 = jnp.float32
BF16 = jnp.bfloat16
I32 = jnp.int32
I16 = jnp.int16

D_MODEL = 1024
CHUNK = 64
ROPE_THETA = 10000.0
RMS_EPS = 1e-6
NEG_INF = -1e30
HEADS = 8
HEAD_DIM = 64
WIDTH = HEADS * HEAD_DIM
IDX_DIM = 64
IDX_ROPE_DIM = 32
TOPK_MAX = 256
B_NOPE = 64
B_ROPE = 32
B_QK = B_NOPE + B_ROPE
B_PAD = 128
Q_LORA = 384
KV_LORA = 256
V_EXT = 80

O_QA, O_KA, O_VA, O_GA = 0, 512, 1024, 1536
O_QI, O_KI, O_WI = 2048, 2560, 2624
O_CQ, O_CKV, O_KR, O_GB, O_END = 2632, 3016, 3272, 3304, 3816

TM = 512
TQ = 256
TK = 512
V7X_VMEM_LIMIT = 56 * 1024 * 1024

INT_MIN = -(2 ** 31)
F32_LOWEST = float(np.finfo(np.float32).min)
LOG2E = float(np.log2(np.e))
_NEG_BITS = int(np.array(NEG_INF, np.float32).view(np.int32))
NEG_KEY = int(np.int32(np.int64(INT_MIN) - np.int64(_NEG_BITS)))


def _rope_rows(a, b, cos, sin):
    return a * cos - b * sin, b * cos + a * sin


def _head_norm_rope_t(xh, gain, cos, sin, n_true, rope_lo, half):
    ssq = jnp.sum(xh * xh, axis=0, keepdims=True)
    y = (xh * lax.rsqrt(ssq * (1.0 / n_true) + RMS_EPS)) * gain
    ra, rb = _rope_rows(y[rope_lo:rope_lo + half], y[rope_lo + half:rope_lo + 2 * half], cos, sin)
    parts = []
    if rope_lo:
        parts.append(y[:rope_lo])
    parts += [ra, rb]
    if rope_lo + 2 * half < y.shape[0]:
        parts.append(y[rope_lo + 2 * half:])
    return jnp.concatenate(parts, axis=0)


def _silu(g):
    return g * (1.0 / (1.0 + jnp.exp(-g)))


def _proj_kernel(x_ref, ng_ref, w_ref, wuq_ref, wuk_ref, wuv_ref,
                 gaq_ref, gak_ref, gql_ref, gkvl_ref, gbq_ref, gbk_ref,
                 c64_ref, s64_ref, c32_ref, s32_ref,
                 qat_ref, ka_ref, vat_ref, gat_ref, qit_ref, ki_ref, wit_ref,
                 qbt_ref, kb_ref, vbt_ref, gbt_ref):
    x = x_ref[0]
    ms = jnp.mean(x * x, axis=-1, keepdims=True)
    xb = ((x * lax.rsqrt(ms + RMS_EPS)) * ng_ref[...]).astype(BF16)

    def proj_t(lo, hi):
        return lax.dot_general(w_ref[lo:hi, :], xb, (((1,), (1,)), ((), ())),
                               preferred_element_type=F32)

    c64, s64 = c64_ref[...], s64_ref[...]
    c32, s32 = c32_ref[...], s32_ref[...]
    nkt = TM // TK
    ones_rows = jnp.ones((V_EXT - HEAD_DIM, TM), F32)

    def store_tiles(ref, val):
        ext = jnp.concatenate(
            [r for h in range(HEADS) for r in (val[h * 64:(h + 1) * 64], ones_rows)], axis=0).astype(BF16)
        for t in range(nkt):
            ref[0, t] = ext[:, t * TK:(t + 1) * TK]

    qa = proj_t(O_QA, O_KA)
    gaq = gaq_ref[...]
    qa = jnp.concatenate(
        [_head_norm_rope_t(qa[h * 64:(h + 1) * 64], gaq, c64, s64, 64, 0, 32) for h in range(HEADS)], axis=0)
    qat_ref[0] = (qa * (HEAD_DIM ** -0.5 * LOG2E)).astype(BF16)

    ka = proj_t(O_KA, O_VA)
    gak = gak_ref[...]
    ka = jnp.concatenate(
        [_head_norm_rope_t(ka[h * 64:(h + 1) * 64], gak, c64, s64, 64, 0, 32) for h in range(HEADS)], axis=0)
    ka_ref[0] = ka.T.astype(BF16)

    vg = proj_t(O_VA, O_QI)
    store_tiles(vat_ref, vg[:WIDTH])
    gat_ref[0] = _silu(vg[WIDTH:]).astype(BF16)

    qi = proj_t(O_QI, O_KI)
    qi_parts = []
    for h in range(HEADS):
        qh = qi[h * 64:(h + 1) * 64]
        ra, rb = _rope_rows(qh[0:16], qh[16:32], c32, s32)
        qi_parts += [ra, rb, qh[32:]]
    qit_ref[0] = jnp.concatenate(qi_parts, axis=0).astype(BF16)

    kw = proj_t(O_KI, O_CQ)
    ra, rb = _rope_rows(kw[0:16], kw[16:32], c32, s32)
    ki = jnp.concatenate([ra, rb, kw[32:64], jnp.zeros((64, TM), F32)], axis=0)
    ki_ref[0] = ki.T.astype(BF16)
    wit_ref[0] = kw[64:72]

    lat = proj_t(O_CQ, O_GB)
    cq = lat[:Q_LORA]
    cq = (cq * lax.rsqrt(jnp.mean(cq * cq, axis=0, keepdims=True) + RMS_EPS)) * gql_ref[...]
    qb = jnp.dot(wuq_ref[...], cq.astype(BF16), preferred_element_type=F32)
    gbq = gbq_ref[...]
    qb = jnp.concatenate(
        [_head_norm_rope_t(qb[h * B_PAD:(h + 1) * B_PAD], gbq, c32, s32, B_QK, B_NOPE, 16)
         for h in range(HEADS)], axis=0)
    qbt_ref[0] = (qb * (B_QK ** -0.5 * LOG2E)).astype(BF16)

    ckv = lat[Q_LORA:Q_LORA + KV_LORA]
    ckv = ((ckv * lax.rsqrt(jnp.mean(ckv * ckv, axis=0, keepdims=True) + RMS_EPS)) * gkvl_ref[...]).astype(BF16)
    kr = lat[Q_LORA + KV_LORA:]
    kn = jnp.dot(wuk_ref[...], ckv, preferred_element_type=F32)
    gbk = gbk_ref[...]
    zpad = jnp.zeros((B_PAD - B_QK, TM), F32)
    kb = jnp.concatenate(
        [_head_norm_rope_t(jnp.concatenate([kn[h * 64:(h + 1) * 64], kr, zpad], axis=0),
                           gbk, c32, s32, B_QK, B_NOPE, 16) for h in range(HEADS)], axis=0)
    kb_ref[0] = kb.T.astype(BF16)
    vb = jnp.dot(wuv_ref[...], ckv, preferred_element_type=F32)
    store_tiles(vbt_ref, vb)

    gbt_ref[0] = _silu(proj_t(O_GB, O_END)).astype(BF16)


def _tile_rows(kt):
    return pl.ds(pl.multiple_of(kt * TK, TK), TK)


def _store_logits(h, kt, s, mrun, s_sc):
    s_sc[h, _tile_rows(kt), :] = s
    return jnp.maximum(mrun, jnp.max(s.reshape(TK // 8, 8, TQ), axis=0))


def _exp_pv(nk, mrun, vt_ref, gate_ref, out_ref, s_sc, acc_sc):
    m = [jnp.max(mrun[h], axis=0, keepdims=True) for h in range(HEADS)]
    acc_sc[...] = jnp.zeros(acc_sc.shape, F32)

    def body(kt, carry):
        for h in range(HEADS):
            p = jnp.exp2((s_sc[h, _tile_rows(kt), :] - m[h]).astype(BF16))
            acc_sc[h] += jnp.dot(vt_ref[0, kt, h * V_EXT:(h + 1) * V_EXT, :], p,
                                 preferred_element_type=F32)
        return carry

    lax.fori_loop(0, nk, body, 0)
    for h in range(HEADS):
        o = acc_sc[h, :HEAD_DIM] * (1.0 / acc_sc[h, HEAD_DIM:HEAD_DIM + 1])
        g = gate_ref[0, h * 64:(h + 1) * 64, :].astype(F32)
        out_ref[0, h * 64:(h + 1) * 64, :] = (o * g).astype(BF16)


def _init_max():
    return tuple(jnp.full((8, TQ), F32_LOWEST, F32) for _ in range(HEADS))


def _num_key_tiles(j):
    return (j * TQ) // TK + 1


def _admissible(kt, j):
    key = lax.broadcasted_iota(I32, (TK, TQ), 0) + kt * TK
    qpos = lax.broadcasted_iota(I32, (1, TQ), 1) + j * TQ
    return key < ((qpos >> 6) + 1) * CHUNK


def _dsa_kernel(seq_len, ki_ref, ka_ref, vat_ref, qit_ref, wit_ref, qat_ref, gat_ref, out_ref,
                key_sc, hi_sc, lo_sc, s_sc, acc_sc):
    j = pl.program_id(1)
    nk = _num_key_tiles(j)
    n_extra = seq_len - nk * TK
    topk = min(TOPK_MAX, seq_len // 4)
    idx_scale = (IDX_DIM * HEADS) ** -0.5
    zeros64 = jnp.zeros((64, TQ), BF16)

    qi = qit_ref[0]
    w = wit_ref[0]
    qpads = [jnp.concatenate([qi[h * 64:(h + 1) * 64], zeros64], axis=0) for h in range(HEADS)]

    def score_body(kt, carry):
        kt_tile = ki_ref[0, _tile_rows(kt), :]
        acc = jnp.zeros((TK, TQ), F32)
        for h in range(HEADS):
            rel = jnp.dot(kt_tile, qpads[h], preferred_element_type=F32)
            acc = acc + jnp.maximum(rel, 0.0) * w[h:h + 1, :]
        score = acc * idx_scale
        score = jnp.where(_admissible(kt, j), score, NEG_INF)
        bits = lax.bitcast_convert_type(score, I32)
        key = jnp.where(bits < 0, INT_MIN - bits, bits)
        key_sc[_tile_rows(kt), :] = key
        hi_sc[_tile_rows(kt), :] = (key >> 16).astype(I16)
        lo_sc[_tile_rows(kt), :] = ((key & 0xFFFF) - 32768).astype(I16)
        return carry

    lax.fori_loop(0, nk, score_body, 0)

    def count(pred):
        def body(kt, c):
            hit = pred(key_sc[_tile_rows(kt), :], kt).astype(I32)
            return c + jnp.sum(hit.reshape(TK // 8, 8, TQ), axis=0)
        part = lax.fori_loop(0, nk, body, jnp.zeros((8, TQ), I32))
        return jnp.sum(part, axis=0, keepdims=True)

    def count16(ref, pred):
        def body(kt, c):
            hit = jnp.where(pred(ref[_tile_rows(kt), :]), jnp.int16(1), jnp.int16(0))
            parts = [hit[r * 16:(r + 1) * 16] for r in range(TK // 16)]
            while len(parts) > 1:
                parts = [a + b for a, b in zip(parts[0::2], parts[1::2])]
            return c + parts[0]
        part = lax.fori_loop(0, nk, body, jnp.zeros((16, TQ), I16))
        return jnp.sum(part.astype(I32), axis=0, keepdims=True)

    def kth_largest16(ref, need, extra_ge):
        def bit_body(i, t_u):
            c_u = t_u | jnp.left_shift(jnp.int32(1), 15 - i)
            c_s = c_u - 32768
            c16 = c_s.astype(I16)
            cnt = count16(ref, lambda x: x >= c16) + extra_ge(c_s)
            return jnp.where(cnt >= need, c_u, t_u)
        return lax.fori_loop(0, 16, bit_body, jnp.zeros((1, TQ), I32)) - 32768

    neg_hi, neg_lo = NEG_KEY >> 16, (NEG_KEY & 0xFFFF) - 32768
    t_hi = kth_largest16(hi_sc, topk, lambda c: jnp.where(neg_hi >= c, n_extra, 0))
    t_hi16 = t_hi.astype(I16)
    above = count16(hi_sc, lambda x: x > t_hi16) + jnp.where(neg_hi > t_hi, n_extra, 0)

    def bucket_body(kt, carry):
        rows = _tile_rows(kt)
        lo_sc[rows, :] = jnp.where(hi_sc[rows, :] == t_hi16, lo_sc[rows, :], jnp.int16(-32768))
        return carry

    lax.fori_loop(0, nk, bucket_body, 0)
    t_lo = kth_largest16(
        lo_sc, topk - above,
        lambda c: jnp.where(jnp.logical_and(t_hi == neg_hi, neg_lo >= c), n_extra, 0))
    thr = t_hi * 65536 + (t_lo + 32768)
    cnt_ge = count(lambda k, kt: k >= thr) + jnp.where(thr <= NEG_KEY, n_extra, 0)
    has_tie = jnp.max(jnp.where(cnt_ge != topk, 1, 0)) > 0

    row_iota = lax.broadcasted_iota(I32, (TK, TQ), 0)

    def tie_limit():
        cnt_gt = count(lambda k, kt: k > thr) + jnp.where(thr < NEG_KEY, n_extra, 0)
        need = topk - cnt_gt

        def idx_body(i, p):
            c = p | jnp.left_shift(jnp.int32(1), 10 - i)
            below = count(lambda k, kt: jnp.logical_and(k == thr, row_iota + kt * TK < c))
            return jnp.where(below < need, c, p)

        return lax.fori_loop(0, 11, idx_body, jnp.zeros((1, TQ), I32))

    p_lim = lax.cond(has_tie, tie_limit, lambda: jnp.full((1, TQ), seq_len, I32))

    qa = qat_ref[0]
    qa_pads = []
    for h in range(HEADS):
        qh = qa[h * 64:(h + 1) * 64]
        qa_pads.append(jnp.concatenate([zeros64, qh] if h % 2 else [qh, zeros64], axis=0))

    def logits_body(kt, mrun):
        k = key_sc[_tile_rows(kt), :]
        sel = jnp.logical_or(k > thr, jnp.logical_and(k == thr, row_iota + kt * TK <= p_lim))
        sel = jnp.logical_and(sel, _admissible(kt, j))
        bias = jnp.where(sel, 0.0, -jnp.inf)
        new = []
        for h in range(HEADS):
            kpair = ka_ref[0, _tile_rows(kt), (h // 2) * 128:(h // 2 + 1) * 128]
            s = jnp.dot(kpair, qa_pads[h], preferred_element_type=F32) + bias
            new.append(_store_logits(h, kt, s, mrun[h], s_sc))
        return tuple(new)

    mrun = lax.fori_loop(0, nk, logits_body, _init_max())
    _exp_pv(nk, mrun, vat_ref, gat_ref, out_ref, s_sc, acc_sc)


def _mla_kernel(kb_ref, vbt_ref, qbt_ref, gbt_ref, out_ref, s_sc, acc_sc):
    j = pl.program_id(1)
    qb = qbt_ref[0]

    def tile(kt, mrun, bias):
        new = []
        for h in range(HEADS):
            kh = kb_ref[0, _tile_rows(kt), h * B_PAD:(h + 1) * B_PAD]
            s = jnp.dot(kh, qb[h * B_PAD:(h + 1) * B_PAD], preferred_element_type=F32)
            if bias is not None:
                s = s + bias
            new.append(_store_logits(h, kt, s, mrun[h], s_sc))
        return tuple(new)

    last = _num_key_tiles(j) - 1
    mrun = lax.fori_loop(0, last, lambda kt, mrun: tile(kt, mrun, None), _init_max())
    mrun = tile(last, mrun, jnp.where(_admissible(last, j), 0.0, -jnp.inf))
    _exp_pv(last + 1, mrun, vbt_ref, gbt_ref, out_ref, s_sc, acc_sc)


def _out_kernel(x_ref, mat_ref, mbt_ref, w_ref, out_ref):
    mixed_t = jnp.concatenate([mat_ref[0], mbt_ref[0]], axis=0)
    y = lax.dot_general(mixed_t, w_ref[...], (((0,), (0,)), ((), ())),
                        preferred_element_type=F32)
    out_ref[0] = x_ref[0] + y


def _rope_tables_t(seq_len, dim):
    half = dim // 2
    freqs = jnp.power(ROPE_THETA, -jnp.arange(half, dtype=F32) * 2.0 / dim)
    ang = jnp.arange(seq_len, dtype=jnp.int32).astype(F32)[None, :] * freqs[:, None]
    return jnp.cos(ang), jnp.sin(ang)


def _params(sem):
    return pltpu.CompilerParams(dimension_semantics=sem, vmem_limit_bytes=V7X_VMEM_LIMIT)


def _layer(h, norm_gain, w_in, a_q_norm, a_k_norm, b_q_latent_norm, b_kv_latent_norm,
           w_uq, w_ukv, b_q_norm, b_k_norm, w_out):
    B, S, D = h.shape
    assert D == D_MODEL and S % TM == 0 and TM % TK == 0 and TK % TQ == 0 and S <= 2048
    nst, nkt, nqb = S // TM, S // TK, S // TQ
    vext = HEADS * V_EXT

    w_t = w_in.T.astype(BF16)
    wuq_t = jnp.pad(w_uq.reshape(Q_LORA, HEADS, B_QK), ((0, 0), (0, 0), (0, B_PAD - B_QK)))
    wuq_t = wuq_t.reshape(Q_LORA, HEADS * B_PAD).T.astype(BF16)
    wukv = w_ukv.reshape(KV_LORA, HEADS, B_NOPE + HEAD_DIM)
    wuk_t = wukv[:, :, :B_NOPE].reshape(KV_LORA, WIDTH).T.astype(BF16)
    wuv_t = wukv[:, :, B_NOPE:].reshape(KV_LORA, WIDTH).T.astype(BF16)
    lanes = lambda g: jnp.broadcast_to(g.astype(F32)[:, None], (g.shape[0], TM))
    pad_b = lambda g: jnp.pad(g, (0, B_PAD - B_QK))
    c64, s64 = _rope_tables_t(S, HEAD_DIM)
    c32, s32 = _rope_tables_t(S, B_ROPE)

    const = lambda shape: pl.BlockSpec(shape, lambda b, i: (0,) * len(shape))
    tok_t = lambda c: pl.BlockSpec((1, c, TM), lambda b, i: (b, 0, i))
    tok = lambda c: pl.BlockSpec((1, TM, c), lambda b, i: (b, i, 0))
    tiles = lambda c: pl.BlockSpec((1, TM // TK, c, TK), lambda b, i: (b, i, 0, 0))
    tab = lambda r: pl.BlockSpec((r, TM), lambda b, i: (0, i))
    sds = jax.ShapeDtypeStruct

    (qat, ka, vat, gat, qit, ki, wit, qbt, kb, vbt, gbt) = pl.pallas_call(
        _proj_kernel,
        grid=(B, nst),
        in_specs=[tok(D), const((1, D)), const((O_END, D)), const((HEADS * B_PAD, Q_LORA)),
                  const((WIDTH, KV_LORA)), const((WIDTH, KV_LORA)),
                  const((64, TM)), const((64, TM)), const((Q_LORA, TM)), const((KV_LORA, TM)),
                  const((B_PAD, TM)), const((B_PAD, TM)),
                  tab(32), tab(32), tab(16), tab(16)],
        out_specs=[tok_t(WIDTH), tok(WIDTH), tiles(vext), tok_t(WIDTH), tok_t(WIDTH), tok(128),
                   tok_t(HEADS), tok_t(HEADS * B_PAD), tok(HEADS * B_PAD), tiles(vext), tok_t(WIDTH)],
        out_shape=[sds((B, WIDTH, S), BF16), sds((B, S, WIDTH), BF16), sds((B, nkt, vext, TK), BF16),
                   sds((B, WIDTH, S), BF16), sds((B, WIDTH, S), BF16), sds((B, S, 128), BF16),
                   sds((B, HEADS, S), F32), sds((B, HEADS * B_PAD, S), BF16),
                   sds((B, S, HEADS * B_PAD), BF16), sds((B, nkt, vext, TK), BF16),
                   sds((B, WIDTH, S), BF16)],
        compiler_params=_params(("arbitrary", "arbitrary")),
        name="proj",
    )(h, norm_gain.reshape(1, D), w_t, wuq_t, wuk_t, wuv_t,
      lanes(a_q_norm), lanes(a_k_norm), lanes(b_q_latent_norm), lanes(b_kv_latent_norm),
      lanes(pad_b(b_q_norm)), lanes(pad_b(b_k_norm)), c64, s64, c32, s32)

    per_b = lambda *shape: pl.BlockSpec((1,) + shape, lambda b, j: (b,) + (0,) * len(shape))
    qblk = lambda c: pl.BlockSpec((1, c, TQ), lambda b, j: (b, 0, j))
    attn_scratch = [pltpu.VMEM((HEADS, S, TQ), F32), pltpu.VMEM((HEADS, V_EXT, TQ), F32)]

    mixed_a = pl.pallas_call(
        functools.partial(_dsa_kernel, S),
        grid=(B, nqb),
        in_specs=[per_b(S, 128), per_b(S, WIDTH), per_b(nkt, vext, TK),
                  qblk(WIDTH), qblk(HEADS), qblk(WIDTH), qblk(WIDTH)],
        out_specs=qblk(WIDTH),
        out_shape=sds((B, WIDTH, S), BF16),
        scratch_shapes=[pltpu.VMEM((S, TQ), I32), pltpu.VMEM((S, TQ), I16),
                        pltpu.VMEM((S, TQ), I16)] + attn_scratch,
        compiler_params=_params(("arbitrary", "arbitrary")),
        name="dsa",
    )(ki, ka, vat, qit, wit, qat, gat)

    mixed_b = pl.pallas_call(
        _mla_kernel,
        grid=(B, nqb),
        in_specs=[per_b(S, HEADS * B_PAD), per_b(nkt, vext, TK), qblk(HEADS * B_PAD), qblk(WIDTH)],
        out_specs=qblk(WIDTH),
        out_shape=sds((B, WIDTH, S), BF16),
        scratch_shapes=attn_scratch,
        compiler_params=_params(("arbitrary", "arbitrary")),
        name="mla",
    )(kb, vbt, qbt, gbt)

    return pl.pallas_call(
        _out_kernel,
        grid=(B, nst),
        in_specs=[tok(D), tok_t(WIDTH), tok_t(WIDTH), const((2 * WIDTH, D))],
        out_specs=tok(D),
        out_shape=sds((B, S, D), F32),
        compiler_params=_params(("arbitrary", "arbitrary")),
        name="outproj",
    )(h, mixed_a, mixed_b, w_out.astype(BF16))


def kernel(x, norm_gain, w_in, a_q_norm, a_k_norm, b_q_latent_norm, b_kv_latent_norm,
           w_uq, w_ukv, b_q_norm, b_k_norm, w_out):
    h = x
    for l in range(norm_gain.shape[0]):
        h = _layer(h, norm_gain[l], w_in[l], a_q_norm[l], a_k_norm[l], b_q_latent_norm[l],
                   b_kv_latent_norm[l], w_uq[l], w_ukv[l], b_q_norm[l], b_k_norm[l], w_out[l])
    return h
```

```python
import functools

import numpy as np
import jax
import jax.numpy as jnp
from jax import lax
from jax.experimental import pallas as pl
from jax.experimental.pallas import tpu as pltpu

F32 = jnp.float32
BF16 = jnp.bfloat16
I32 = jnp.int32
I16 = jnp.int16

D_MODEL = 1024
CHUNK = 64
ROPE_THETA = 10000.0
RMS_EPS = 1e-6
NEG_INF = -1e30
HEADS = 8
HEAD_DIM = 64
WIDTH = HEADS * HEAD_DIM
IDX_DIM = 64
IDX_ROPE_DIM = 32
TOPK_MAX = 256
B_NOPE = 64
B_ROPE = 32
B_QK = B_NOPE + B_ROPE
B_PAD = 128
Q_LORA = 384
KV_LORA = 256
V_EXT = 80

O_QA, O_KA, O_VA, O_GA = 0, 512, 1024, 1536
O_QI, O_KI, O_WI = 2048, 2560, 2624
O_CQ, O_CKV, O_KR, O_GB, O_END = 2632, 3016, 3272, 3304, 3816

TM = 512
TQ = 256
TK = 512
V7X_VMEM_LIMIT = 56 * 1024 * 1024

INT_MIN = -(2 ** 31)
F32_LOWEST = float(np.finfo(np.float32).min)
LOG2E = float(np.log2(np.e))
KEY_POS_INF = 0x7F800000
KEY_NEG_INF = INT_MIN + 0x00800000
_NEG_BITS = int(np.array(NEG_INF, np.float32).view(np.int32))
NEG_KEY = int(np.int32(np.int64(INT_MIN) - np.int64(_NEG_BITS)))


def _rope_rows(a, b, cos, sin):
    return a * cos - b * sin, b * cos + a * sin


def _head_norm_rope_t(xh, gain, cos, sin, n_true, rope_lo, half):
    ssq = jnp.sum(xh * xh, axis=0, keepdims=True)
    y = (xh * lax.rsqrt(ssq * (1.0 / n_true) + RMS_EPS)) * gain
    ra, rb = _rope_rows(y[rope_lo:rope_lo + half], y[rope_lo + half:rope_lo + 2 * half], cos, sin)
    parts = []
    if rope_lo:
        parts.append(y[:rope_lo])
    parts += [ra, rb]
    if rope_lo + 2 * half < y.shape[0]:
        parts.append(y[rope_lo + 2 * half:])
    return jnp.concatenate(parts, axis=0)


def _silu(g):
    return g * (1.0 / (1.0 + jnp.exp(-g)))


def _key_to_float(key):
    key = jnp.clip(key, KEY_NEG_INF, KEY_POS_INF)
    return lax.bitcast_convert_type(jnp.where(key < 0, INT_MIN - key, key), F32)


def _proj_kernel(x_ref, ng_ref, w_ref, wuq_ref, wuk_ref, wuv_ref,
                 gaq_ref, gak_ref, gql_ref, gkvl_ref, gbq_ref, gbk_ref,
                 c64_ref, s64_ref, c32_ref, s32_ref,
                 qat_ref, ka_ref, vat_ref, gat_ref, qit_ref, ki_ref, wit_ref,
                 qbt_ref, kb_ref, vbt_ref, gbt_ref):
    x = x_ref[0]
    ms = jnp.mean(x * x, axis=-1, keepdims=True)
    xb = ((x * lax.rsqrt(ms + RMS_EPS)) * ng_ref[...]).astype(BF16)

    def proj_t(lo, hi):
        return lax.dot_general(w_ref[lo:hi, :], xb, (((1,), (1,)), ((), ())),
                               preferred_element_type=F32)

    c64, s64 = c64_ref[...], s64_ref[...]
    c32, s32 = c32_ref[...], s32_ref[...]
    nkt = TM // TK
    ones_rows = jnp.ones((V_EXT - HEAD_DIM, TM), F32)

    def store_tiles(ref, val):
        ext = jnp.concatenate(
            [r for h in range(HEADS) for r in (val[h * 64:(h + 1) * 64], ones_rows)], axis=0).astype(BF16)
        for t in range(nkt):
            ref[0, t] = ext[:, t * TK:(t + 1) * TK]

    qa = proj_t(O_QA, O_KA)
    gaq = gaq_ref[...]
    qa = jnp.concatenate(
        [_head_norm_rope_t(qa[h * 64:(h + 1) * 64], gaq, c64, s64, 64, 0, 32) for h in range(HEADS)], axis=0)
    qat_ref[0] = (qa * (HEAD_DIM ** -0.5 * LOG2E)).astype(BF16)

    ka = proj_t(O_KA, O_VA)
    gak = gak_ref[...]
    ka = jnp.concatenate(
        [_head_norm_rope_t(ka[h * 64:(h + 1) * 64], gak, c64, s64, 64, 0, 32) for h in range(HEADS)], axis=0)
    ka_ref[0] = ka.T.astype(BF16)

    vg = proj_t(O_VA, O_QI)
    store_tiles(vat_ref, vg[:WIDTH])
    gat_ref[0] = _silu(vg[WIDTH:]).astype(BF16)

    qi = proj_t(O_QI, O_KI)
    qi_parts = []
    for h in range(HEADS):
        qh = qi[h * 64:(h + 1) * 64]
        ra, rb = _rope_rows(qh[0:16], qh[16:32], c32, s32)
        qi_parts += [ra, rb, qh[32:]]
    qit_ref[0] = jnp.concatenate(qi_parts, axis=0).astype(BF16)

    kw = proj_t(O_KI, O_CQ)
    ra, rb = _rope_rows(kw[0:16], kw[16:32], c32, s32)
    ki = jnp.concatenate([ra, rb, kw[32:64], jnp.zeros((64, TM), F32)], axis=0)
    ki_ref[0] = ki.T.astype(BF16)
    wit_ref[0] = kw[64:72]

    lat = proj_t(O_CQ, O_GB)
    cq = lat[:Q_LORA]
    cq = (cq * lax.rsqrt(jnp.mean(cq * cq, axis=0, keepdims=True) + RMS_EPS)) * gql_ref[...]
    qb = jnp.dot(wuq_ref[...], cq.astype(BF16), preferred_element_type=F32)
    gbq = gbq_ref[...]
    qb = jnp.concatenate(
        [_head_norm_rope_t(qb[h * B_PAD:(h + 1) * B_PAD], gbq, c32, s32, B_QK, B_NOPE, 16)
         for h in range(HEADS)], axis=0)
    qbt_ref[0] = (qb * (B_QK ** -0.5 * LOG2E)).astype(BF16)

    ckv = lat[Q_LORA:Q_LORA + KV_LORA]
    ckv = ((ckv * lax.rsqrt(jnp.mean(ckv * ckv, axis=0, keepdims=True) + RMS_EPS)) * gkvl_ref[...]).astype(BF16)
    kr = lat[Q_LORA + KV_LORA:]
    kn = jnp.dot(wuk_ref[...], ckv, preferred_element_type=F32)
    gbk = gbk_ref[...]
    zpad = jnp.zeros((B_PAD - B_QK, TM), F32)
    kb = jnp.concatenate(
        [_head_norm_rope_t(jnp.concatenate([kn[h * 64:(h + 1) * 64], kr, zpad], axis=0),
                           gbk, c32, s32, B_QK, B_NOPE, 16) for h in range(HEADS)], axis=0)
    kb_ref[0] = kb.T.astype(BF16)
    vb = jnp.dot(wuv_ref[...], ckv, preferred_element_type=F32)
    store_tiles(vbt_ref, vb)

    gbt_ref[0] = _silu(proj_t(O_GB, O_END)).astype(BF16)


def _rows(kt):
    return slice(kt * TK, (kt + 1) * TK)


def _logits_pass(nk, qk_fn, bias_fn, s_sc):
    mrun = [jnp.full((8, TQ), F32_LOWEST, F32) for _ in range(HEADS)]
    for kt in range(nk):
        bias = bias_fn(kt)
        for h in range(HEADS):
            s = qk_fn(h, kt)
            if bias is not None:
                s = s + bias
            s_sc[h, _rows(kt), :] = s
            mrun[h] = jnp.maximum(mrun[h], jnp.max(s.reshape(TK // 8, 8, TQ), axis=0))
    return [jnp.max(mrun[h], axis=0, keepdims=True) for h in range(HEADS)]


def _exp_pv(nk, m, vt_ref, gate_ref, out_ref, s_sc):
    for h in range(HEADS):
        acc = jnp.zeros((V_EXT, TQ), F32)
        for kt in range(nk):
            p = jnp.exp2((s_sc[h, _rows(kt), :] - m[h]).astype(BF16))
            acc = acc + jnp.dot(vt_ref[0, kt, h * V_EXT:(h + 1) * V_EXT, :], p,
                                preferred_element_type=F32)
        o = acc[:HEAD_DIM] * (1.0 / acc[HEAD_DIM:HEAD_DIM + 1])
        g = gate_ref[0, h * 64:(h + 1) * 64, :].astype(F32)
        out_ref[0, h * 64:(h + 1) * 64, :] = (o * g).astype(BF16)


def _num_key_tiles(j):
    return (j * TQ) // TK + 1


def _for_each_tile_count(seq_len, j, run):
    nk = _num_key_tiles(j)
    for n in range(1, seq_len // TK + 1):
        pl.when(nk == n)(functools.partial(run, n))


def _admissible(kt, j):
    key = lax.broadcasted_iota(I32, (TK, TQ), 0) + kt * TK
    qpos = lax.broadcasted_iota(I32, (1, TQ), 1) + j * TQ
    return key < ((qpos >> 6) + 1) * CHUNK


def _dsa_kernel(seq_len, ki_ref, ka_ref, vat_ref, qit_ref, wit_ref, qat_ref, gat_ref, out_ref,
                sc_sc, hi_sc, lo_sc, s_sc):
    j = pl.program_id(1)
    topk = min(TOPK_MAX, seq_len // 4)
    idx_scale = (IDX_DIM * HEADS) ** -0.5
    neg_hi, neg_lo = NEG_KEY >> 16, (NEG_KEY & 0xFFFF) - 32768

    def run(nk):
        n_extra = seq_len - nk * TK
        zeros64 = jnp.zeros((64, TQ), BF16)
        row_iota = lax.broadcasted_iota(I32, (TK, TQ), 0)

        qi = qit_ref[0]
        w = wit_ref[0]
        qpads = [jnp.concatenate([qi[h * 64:(h + 1) * 64], zeros64], axis=0) for h in range(HEADS)]
        for kt in range(nk):
            kt_tile = ki_ref[0, _rows(kt), :]
            acc = jnp.zeros((TK, TQ), F32)
            for h in range(HEADS):
                rel = jnp.dot(kt_tile, qpads[h], preferred_element_type=F32)
                acc = acc + jnp.maximum(rel, 0.0) * w[h:h + 1, :]
            score = jnp.where(_admissible(kt, j), acc * idx_scale, NEG_INF)
            sc_sc[_rows(kt), :] = score
            bits = lax.bitcast_convert_type(score, I32)
            key = jnp.where(bits < 0, INT_MIN - bits, bits)
            hi_sc[_rows(kt), :] = (key >> 16).astype(I16)
            lo_sc[_rows(kt), :] = ((key & 0xFFFF) - 32768).astype(I16)

        def count(pred):
            part = jnp.zeros((8, TQ), I32)
            for kt in range(nk):
                hit = pred(sc_sc[_rows(kt), :], kt).astype(I32)
                part = part + jnp.sum(hit.reshape(TK // 8, 8, TQ), axis=0)
            return jnp.sum(part, axis=0, keepdims=True)

        def count_ge(cand):
            return count(lambda s, kt: s >= cand) + jnp.where(NEG_INF >= cand, n_extra, 0)

        def count16(ref, pred):
            parts = []
            for kt in range(nk):
                hit = jnp.where(pred(ref[_rows(kt), :]), jnp.int16(1), jnp.int16(0))
                parts += [hit[r * 16:(r + 1) * 16] for r in range(TK // 16)]
            while len(parts) > 1:
                odd = parts[-1:] if len(parts) % 2 else []
                parts = [a + b for a, b in zip(parts[0::2], parts[1::2])] + odd
            return jnp.sum(parts[0].astype(I32), axis=0, keepdims=True)

        def kth_largest16(ref, need, extra_ge):
            def bit_body(i, t_u):
                c_u = t_u | jnp.left_shift(jnp.int32(1), 15 - i)
                c_s = c_u - 32768
                c16 = c_s.astype(I16)
                cnt = count16(ref, lambda x: x >= c16) + extra_ge(c_s)
                return jnp.where(cnt >= need, c_u, t_u)
            return lax.fori_loop(0, 16, bit_body, jnp.zeros((1, TQ), I32)) - 32768

        t_hi = kth_largest16(hi_sc, topk, lambda c: jnp.where(neg_hi >= c, n_extra, 0))
        t_hi16 = t_hi.astype(I16)
        above = count16(hi_sc, lambda x: x > t_hi16) + jnp.where(neg_hi > t_hi, n_extra, 0)
        for kt in range(nk):
            lo_sc[_rows(kt), :] = jnp.where(hi_sc[_rows(kt), :] == t_hi16, lo_sc[_rows(kt), :],
                                            jnp.int16(-32768))
        t_lo = kth_largest16(
            lo_sc, topk - above,
            lambda c: jnp.where(jnp.logical_and(t_hi == neg_hi, neg_lo >= c), n_extra, 0))
        key_fast = t_hi * 65536 + (t_lo + 32768)

        cnt_fast = count_ge(_key_to_float(key_fast))
        cnt_next = count_ge(_key_to_float(jnp.minimum(key_fast, KEY_POS_INF) + 1))
        certified = jnp.min(jnp.where(jnp.logical_and(cnt_fast >= topk, cnt_next < topk), 1, 0)) > 0

        def float_search():
            def bit_body(i, t_u):
                c_u = t_u | jnp.left_shift(jnp.int32(1), 31 - i)
                return jnp.where(count_ge(_key_to_float(c_u ^ INT_MIN)) >= topk, c_u, t_u)
            key = lax.fori_loop(0, 32, bit_body, jnp.zeros((1, TQ), I32)) ^ INT_MIN
            return key, count_ge(_key_to_float(key))

        thr_key, cnt_ge = lax.cond(certified, lambda: (key_fast, cnt_fast), float_search)
        thr = _key_to_float(thr_key)
        has_tie = jnp.max(jnp.where(cnt_ge != topk, 1, 0)) > 0

        def tie_limit():
            cnt_gt = count(lambda s, kt: s > thr) + jnp.where(NEG_INF > thr, n_extra, 0)
            need = topk - cnt_gt

            def idx_body(i, p):
                c = p | jnp.left_shift(jnp.int32(1), 10 - i)
                below = count(lambda s, kt: jnp.logical_and(s == thr, row_iota + kt * TK < c))
                return jnp.where(below < need, c, p)

            return lax.fori_loop(0, 11, idx_body, jnp.zeros((1, TQ), I32))

        p_lim = lax.cond(has_tie, tie_limit, lambda: jnp.full((1, TQ), seq_len, I32))

        qa = qat_ref[0]
        qa_pads = []
        for h in range(HEADS):
            qh = qa[h * 64:(h + 1) * 64]
            qa_pads.append(jnp.concatenate([zeros64, qh] if h % 2 else [qh, zeros64], axis=0))

        def bias_fn(kt):
            s = sc_sc[_rows(kt), :]
            sel = jnp.logical_or(s > thr, jnp.logical_and(s == thr, row_iota + kt * TK <= p_lim))
            sel = jnp.logical_and(sel, _admissible(kt, j))
            return jnp.where(sel, 0.0, -jnp.inf)

        def qk_fn(h, kt):
            kpair = ka_ref[0, _rows(kt), (h // 2) * 128:(h // 2 + 1) * 128]
            return jnp.dot(kpair, qa_pads[h], preferred_element_type=F32)

        m = _logits_pass(nk, qk_fn, bias_fn, s_sc)
        _exp_pv(nk, m, vat_ref, gat_ref, out_ref, s_sc)

    _for_each_tile_count(seq_len, j, run)


def _mla_kernel(seq_len, kb_ref, vbt_ref, qbt_ref, gbt_ref, out_ref, s_sc):
    j = pl.program_id(1)

    def run(nk):
        qb = qbt_ref[0]

        def bias_fn(kt):
            return jnp.where(_admissible(kt, j), 0.0, -jnp.inf) if kt == nk - 1 else None

        def qk_fn(h, kt):
            kh = kb_ref[0, _rows(kt), h * B_PAD:(h + 1) * B_PAD]
            return jnp.dot(kh, qb[h * B_PAD:(h + 1) * B_PAD], preferred_element_type=F32)

        m = _logits_pass(nk, qk_fn, bias_fn, s_sc)
        _exp_pv(nk, m, vbt_ref, gbt_ref, out_ref, s_sc)

    _for_each_tile_count(seq_len, j, run)


def _out_kernel(x_ref, mat_ref, mbt_ref, w_ref, out_ref):
    mixed_t = jnp.concatenate([mat_ref[0], mbt_ref[0]], axis=0)
    y = lax.dot_general(mixed_t, w_ref[...], (((0,), (0,)), ((), ())),
                        preferred_element_type=F32)
    out_ref[0] = x_ref[0] + y


def _rope_tables_t(seq_len, dim):
    half = dim // 2
    freqs = jnp.power(ROPE_THETA, -jnp.arange(half, dtype=F32) * 2.0 / dim)
    ang = jnp.arange(seq_len, dtype=jnp.int32).astype(F32)[None, :] * freqs[:, None]
    return jnp.cos(ang), jnp.sin(ang)


def _params(sem):
    return pltpu.CompilerParams(dimension_semantics=sem, vmem_limit_bytes=V7X_VMEM_LIMIT)


def _layer(h, norm_gain, w_in, a_q_norm, a_k_norm, b_q_latent_norm, b_kv_latent_norm,
           w_uq, w_ukv, b_q_norm, b_k_norm, w_out):
    B, S, D = h.shape
    assert D == D_MODEL and S % TM == 0 and TM % TK == 0 and TK % TQ == 0 and S <= 2048
    nst, nkt, nqb = S // TM, S // TK, S // TQ
    vext = HEADS * V_EXT

    w_t = w_in.T.astype(BF16)
    wuq_t = jnp.pad(w_uq.reshape(Q_LORA, HEADS, B_QK), ((0, 0), (0, 0), (0, B_PAD - B_QK)))
    wuq_t = wuq_t.reshape(Q_LORA, HEADS * B_PAD).T.astype(BF16)
    wukv = w_ukv.reshape(KV_LORA, HEADS, B_NOPE + HEAD_DIM)
    wuk_t = wukv[:, :, :B_NOPE].reshape(KV_LORA, WIDTH).T.astype(BF16)
    wuv_t = wukv[:, :, B_NOPE:].reshape(KV_LORA, WIDTH).T.astype(BF16)
    lanes = lambda g: jnp.broadcast_to(g.astype(F32)[:, None], (g.shape[0], TM))
    pad_b = lambda g: jnp.pad(g, (0, B_PAD - B_QK))
    c64, s64 = _rope_tables_t(S, HEAD_DIM)
    c32, s32 = _rope_tables_t(S, B_ROPE)

    const = lambda shape: pl.BlockSpec(shape, lambda b, i: (0,) * len(shape))
    tok_t = lambda c: pl.BlockSpec((1, c, TM), lambda b, i: (b, 0, i))
    tok = lambda c: pl.BlockSpec((1, TM, c), lambda b, i: (b, i, 0))
    tiles = lambda c: pl.BlockSpec((1, TM // TK, c, TK), lambda b, i: (b, i, 0, 0))
    tab = lambda r: pl.BlockSpec((r, TM), lambda b, i: (0, i))
    sds = jax.ShapeDtypeStruct

    (qat, ka, vat, gat, qit, ki, wit, qbt, kb, vbt, gbt) = pl.pallas_call(
        _proj_kernel,
        grid=(B, nst),
        in_specs=[tok(D), const((1, D)), const((O_END, D)), const((HEADS * B_PAD, Q_LORA)),
                  const((WIDTH, KV_LORA)), const((WIDTH, KV_LORA)),
                  const((64, TM)), const((64, TM)), const((Q_LORA, TM)), const((KV_LORA, TM)),
                  const((B_PAD, TM)), const((B_PAD, TM)),
                  tab(32), tab(32), tab(16), tab(16)],
        out_specs=[tok_t(WIDTH), tok(WIDTH), tiles(vext), tok_t(WIDTH), tok_t(WIDTH), tok(128),
                   tok_t(HEADS), tok_t(HEADS * B_PAD), tok(HEADS * B_PAD), tiles(vext), tok_t(WIDTH)],
        out_shape=[sds((B, WIDTH, S), BF16), sds((B, S, WIDTH), BF16), sds((B, nkt, vext, TK), BF16),
                   sds((B, WIDTH, S), BF16), sds((B, WIDTH, S), BF16), sds((B, S, 128), BF16),
                   sds((B, HEADS, S), F32), sds((B, HEADS * B_PAD, S), BF16),
                   sds((B, S, HEADS * B_PAD), BF16), sds((B, nkt, vext, TK), BF16),
                   sds((B, WIDTH, S), BF16)],
        compiler_params=_params(("arbitrary", "arbitrary")),
        name="proj",
    )(h, norm_gain.reshape(1, D), w_t, wuq_t, wuk_t, wuv_t,
      lanes(a_q_norm), lanes(a_k_norm), lanes(b_q_latent_norm), lanes(b_kv_latent_norm),
      lanes(pad_b(b_q_norm)), lanes(pad_b(b_k_norm)), c64, s64, c32, s32)

    per_b = lambda *shape: pl.BlockSpec((1,) + shape, lambda b, j: (b,) + (0,) * len(shape))
    qblk = lambda c: pl.BlockSpec((1, c, TQ), lambda b, j: (b, 0, j))
    attn_scratch = [pltpu.VMEM((HEADS, S, TQ), F32)]

    mixed_a = pl.pallas_call(
        functools.partial(_dsa_kernel, S),
        grid=(B, nqb),
        in_specs=[per_b(S, 128), per_b(S, WIDTH), per_b(nkt, vext, TK),
                  qblk(WIDTH), qblk(HEADS), qblk(WIDTH), qblk(WIDTH)],
        out_specs=qblk(WIDTH),
        out_shape=sds((B, WIDTH, S), BF16),
        scratch_shapes=[pltpu.VMEM((S, TQ), F32), pltpu.VMEM((S, TQ), I16),
                        pltpu.VMEM((S, TQ), I16)] + attn_scratch,
        compiler_params=_params(("arbitrary", "arbitrary")),
        name="dsa",
    )(ki, ka, vat, qit, wit, qat, gat)

    mixed_b = pl.pallas_call(
        functools.partial(_mla_kernel, S),
        grid=(B, nqb),
        in_specs=[per_b(S, HEADS * B_PAD), per_b(nkt, vext, TK), qblk(HEADS * B_PAD), qblk(WIDTH)],
        out_specs=qblk(WIDTH),
        out_shape=sds((B, WIDTH, S), BF16),
        scratch_shapes=attn_scratch,
        compiler_params=_params(("arbitrary", "arbitrary")),
        name="mla",
    )(kb, vbt, qbt, gbt)

    return pl.pallas_call(
        _out_kernel,
        grid=(B, nst),
        in_specs=[tok(D), tok_t(WIDTH), tok_t(WIDTH), const((2 * WIDTH, D))],
        out_specs=tok(D),
        out_shape=sds((B, S, D), F32),
        compiler_params=_params(("arbitrary", "arbitrary")),
        name="outproj",
    )(h, mixed_a, mixed_b, w_out.astype(BF16))


def kernel(x, norm_gain, w_in, a_q_norm, a_k_norm, b_q_latent_norm, b_kv_latent_norm,
           w_uq, w_ukv, b_q_norm, b_k_norm, w_out):
    h = x
    for l in range(norm_gain.shape[0]):
        h = _layer(h, norm_gain[l], w_in[l], a_q_norm[l], a_k_norm[l], b_q_latent_norm[l],
                   b_kv_latent_norm[l], w_uq[l], w_ukv[l], b_q_norm[l], b_k_norm[l], w_out[l])
    return h
```

```python
import functools

import numpy as np
import jax
import jax.numpy as jnp
from jax import lax
from jax.experimental import pallas as pl
from jax.experimental.pallas import tpu as pltpu

F32 = jnp.float32
BF16 = jnp.bfloat16
I32 = jnp.int32
I16 = jnp.int16

D_MODEL = 1024
CHUNK = 64
ROPE_THETA = 10000.0
RMS_EPS = 1e-6
NEG_INF = -1e30
HEADS = 8
HEAD_DIM = 64
WIDTH = HEADS * HEAD_DIM
IDX_DIM = 64
IDX_ROPE_DIM = 32
TOPK_MAX = 256
B_NOPE = 64
B_ROPE = 32
B_QK = B_NOPE + B_ROPE
B_PAD = 128
Q_LORA = 384
KV_LORA = 256
V_EXT = 80

O_QA, O_KA, O_VA, O_GA = 0, 512, 1024, 1536
O_QI, O_KI, O_WI = 2048, 2560, 2624
O_CQ, O_CKV, O_KR, O_GB, O_END = 2632, 3016, 3272, 3304, 3816

TM = 512
TQ = 256
TK = 512
V7X_VMEM_LIMIT = 56 * 1024 * 1024

INT_MIN = -(2 ** 31)
F32_LOWEST = float(np.finfo(np.float32).min)
LOG2E = float(np.log2(np.e))
KEY_POS_INF = 0x7F800000
KEY_NEG_INF = INT_MIN + 0x00800000
KEY_MIN_NORMAL = 0x00800000
_NEG_BITS = int(np.array(NEG_INF, np.float32).view(np.int32))
NEG_KEY = int(np.int32(np.int64(INT_MIN) - np.int64(_NEG_BITS)))


def _rope_rows(a, b, cos, sin):
    return a * cos - b * sin, b * cos + a * sin


def _head_norm_rope_t(xh, gain, cos, sin, n_true, rope_lo, half):
    ssq = jnp.sum(xh * xh, axis=0, keepdims=True)
    y = (xh * lax.rsqrt(ssq * (1.0 / n_true) + RMS_EPS)) * gain
    ra, rb = _rope_rows(y[rope_lo:rope_lo + half], y[rope_lo + half:rope_lo + 2 * half], cos, sin)
    parts = []
    if rope_lo:
        parts.append(y[:rope_lo])
    parts += [ra, rb]
    if rope_lo + 2 * half < y.shape[0]:
        parts.append(y[rope_lo + 2 * half:])
    return jnp.concatenate(parts, axis=0)


def _silu(g):
    return g * (1.0 / (1.0 + jnp.exp(-g)))


def _key_to_float(key):
    key = jnp.clip(key, KEY_NEG_INF, KEY_POS_INF)
    key = jnp.where(jnp.abs(key) < KEY_MIN_NORMAL, 0, key)
    return lax.bitcast_convert_type(jnp.where(key < 0, INT_MIN - key, key), F32)


def _next_key_up(key):
    nxt = jnp.minimum(key, KEY_POS_INF) + 1
    return jnp.where(jnp.logical_and(nxt > 0, nxt < KEY_MIN_NORMAL), KEY_MIN_NORMAL, nxt)


def _proj_kernel(x_ref, ng_ref, w_ref, wuq_ref, wuk_ref, wuv_ref,
                 gaq_ref, gak_ref, gql_ref, gkvl_ref, gbq_ref, gbk_ref,
                 c64_ref, s64_ref, c32_ref, s32_ref,
                 qat_ref, ka_ref, vat_ref, gat_ref, qit_ref, ki_ref, wit_ref,
                 qbt_ref, kb_ref, vbt_ref, gbt_ref):
    x = x_ref[0]
    ms = jnp.mean(x * x, axis=-1, keepdims=True)
    xb = ((x * lax.rsqrt(ms + RMS_EPS)) * ng_ref[...]).astype(BF16)

    def proj_t(lo, hi):
        return lax.dot_general(w_ref[lo:hi, :], xb, (((1,), (1,)), ((), ())),
                               preferred_element_type=F32)

    c64, s64 = c64_ref[...], s64_ref[...]
    c32, s32 = c32_ref[...], s32_ref[...]
    nkt = TM // TK
    ones_rows = jnp.ones((V_EXT - HEAD_DIM, TM), F32)

    def store_tiles(ref, val):
        ext = jnp.concatenate(
            [r for h in range(HEADS) for r in (val[h * 64:(h + 1) * 64], ones_rows)], axis=0).astype(BF16)
        for t in range(nkt):
            ref[0, t] = ext[:, t * TK:(t + 1) * TK]

    qa = proj_t(O_QA, O_KA)
    gaq = gaq_ref[...]
    qa = jnp.concatenate(
        [_head_norm_rope_t(qa[h * 64:(h + 1) * 64], gaq, c64, s64, 64, 0, 32) for h in range(HEADS)], axis=0)
    qat_ref[0] = (qa * (HEAD_DIM ** -0.5 * LOG2E)).astype(BF16)

    ka = proj_t(O_KA, O_VA)
    gak = gak_ref[...]
    ka = jnp.concatenate(
        [_head_norm_rope_t(ka[h * 64:(h + 1) * 64], gak, c64, s64, 64, 0, 32) for h in range(HEADS)], axis=0)
    ka_ref[0] = ka.T.astype(BF16)

    vg = proj_t(O_VA, O_QI)
    store_tiles(vat_ref, vg[:WIDTH])
    gat_ref[0] = _silu(vg[WIDTH:]).astype(BF16)

    qi = proj_t(O_QI, O_KI)
    qi_parts = []
    for h in range(HEADS):
        qh = qi[h * 64:(h + 1) * 64]
        ra, rb = _rope_rows(qh[0:16], qh[16:32], c32, s32)
        qi_parts += [ra, rb, qh[32:]]
    qit_ref[0] = jnp.concatenate(qi_parts, axis=0).astype(BF16)

    kw = proj_t(O_KI, O_CQ)
    ra, rb = _rope_rows(kw[0:16], kw[16:32], c32, s32)
    ki = jnp.concatenate([ra, rb, kw[32:64], jnp.zeros((64, TM), F32)], axis=0)
    ki_ref[0] = ki.T.astype(BF16)
    wit_ref[0] = kw[64:72]

    lat = proj_t(O_CQ, O_GB)
    cq = lat[:Q_LORA]
    cq = (cq * lax.rsqrt(jnp.mean(cq * cq, axis=0, keepdims=True) + RMS_EPS)) * gql_ref[...]
    qb = jnp.dot(wuq_ref[...], cq.astype(BF16), preferred_element_type=F32)
    gbq = gbq_ref[...]
    qb = jnp.concatenate(
        [_head_norm_rope_t(qb[h * B_PAD:(h + 1) * B_PAD], gbq, c32, s32, B_QK, B_NOPE, 16)
         for h in range(HEADS)], axis=0)
    qbt_ref[0] = (qb * (B_QK ** -0.5 * LOG2E)).astype(BF16)

    ckv = lat[Q_LORA:Q_LORA + KV_LORA]
    ckv = ((ckv * lax.rsqrt(jnp.mean(ckv * ckv, axis=0, keepdims=True) + RMS_EPS)) * gkvl_ref[...]).astype(BF16)
    kr = lat[Q_LORA + KV_LORA:]
    kn = jnp.dot(wuk_ref[...], ckv, preferred_element_type=F32)
    gbk = gbk_ref[...]
    zpad = jnp.zeros((B_PAD - B_QK, TM), F32)
    kb = jnp.concatenate(
        [_head_norm_rope_t(jnp.concatenate([kn[h * 64:(h + 1) * 64], kr, zpad], axis=0),
                           gbk, c32, s32, B_QK, B_NOPE, 16) for h in range(HEADS)], axis=0)
    kb_ref[0] = kb.T.astype(BF16)
    vb = jnp.dot(wuv_ref[...], ckv, preferred_element_type=F32)
    store_tiles(vbt_ref, vb)

    gbt_ref[0] = _silu(proj_t(O_GB, O_END)).astype(BF16)


def _rows(kt):
    return slice(kt * TK, (kt + 1) * TK)


def _logits_pass(nk, qk_fn, bias_fn, s_sc):
    mrun = [jnp.full((8, TQ), F32_LOWEST, F32) for _ in range(HEADS)]
    for kt in range(nk):
        bias = bias_fn(kt)
        for h in range(HEADS):
            s = qk_fn(h, kt)
            if bias is not None:
                s = s + bias
            s_sc[h, _rows(kt), :] = s
            mrun[h] = jnp.maximum(mrun[h], jnp.max(s.reshape(TK // 8, 8, TQ), axis=0))
    return [jnp.max(mrun[h], axis=0, keepdims=True) for h in range(HEADS)]


def _exp_pv(nk, m, vt_ref, gate_ref, out_ref, s_sc):
    for h in range(HEADS):
        acc = jnp.zeros((V_EXT, TQ), F32)
        for kt in range(nk):
            p = jnp.exp2((s_sc[h, _rows(kt), :] - m[h]).astype(BF16))
            acc = acc + jnp.dot(vt_ref[0, kt, h * V_EXT:(h + 1) * V_EXT, :], p,
                                preferred_element_type=F32)
        o = acc[:HEAD_DIM] * (1.0 / acc[HEAD_DIM:HEAD_DIM + 1])
        g = gate_ref[0, h * 64:(h + 1) * 64, :].astype(F32)
        out_ref[0, h * 64:(h + 1) * 64, :] = (o * g).astype(BF16)


def _num_key_tiles(j):
    return (j * TQ) // TK + 1


def _for_each_tile_count(seq_len, j, run):
    nk = _num_key_tiles(j)
    for n in range(1, seq_len // TK + 1):
        pl.when(nk == n)(functools.partial(run, n))


def _admissible(kt, j):
    key = lax.broadcasted_iota(I32, (TK, TQ), 0) + kt * TK
    qpos = lax.broadcasted_iota(I32, (1, TQ), 1) + j * TQ
    return key < ((qpos >> 6) + 1) * CHUNK


def _dsa_kernel(seq_len, ki_ref, ka_ref, vat_ref, qit_ref, wit_ref, qat_ref, gat_ref, out_ref,
                sc_sc, hi_sc, lo_sc, s_sc):
    j = pl.program_id(1)
    topk = min(TOPK_MAX, seq_len // 4)
    idx_scale = (IDX_DIM * HEADS) ** -0.5
    neg_hi, neg_lo = NEG_KEY >> 16, (NEG_KEY & 0xFFFF) - 32768

    def run(nk):
        n_extra = seq_len - nk * TK
        zeros64 = jnp.zeros((64, TQ), BF16)
        row_iota = lax.broadcasted_iota(I32, (TK, TQ), 0)

        qi = qit_ref[0]
        w = wit_ref[0]
        qpads = [jnp.concatenate([qi[h * 64:(h + 1) * 64], zeros64], axis=0) for h in range(HEADS)]
        for kt in range(nk):
            kt_tile = ki_ref[0, _rows(kt), :]
            acc = jnp.zeros((TK, TQ), F32)
            for h in range(HEADS):
                rel = jnp.dot(kt_tile, qpads[h], preferred_element_type=F32)
                acc = acc + jnp.maximum(rel, 0.0) * w[h:h + 1, :]
            score = jnp.where(_admissible(kt, j), acc * idx_scale, NEG_INF)
            sc_sc[_rows(kt), :] = score
            bits = lax.bitcast_convert_type(score, I32)
            key = jnp.where(bits < 0, INT_MIN - bits, bits)
            hi_sc[_rows(kt), :] = (key >> 16).astype(I16)
            lo_sc[_rows(kt), :] = ((key & 0xFFFF) - 32768).astype(I16)

        def count(pred):
            part = jnp.zeros((8, TQ), I32)
            for kt in range(nk):
                hit = pred(sc_sc[_rows(kt), :], kt).astype(I32)
                part = part + jnp.sum(hit.reshape(TK // 8, 8, TQ), axis=0)
            return jnp.sum(part, axis=0, keepdims=True)

        def count_ge(cand):
            return count(lambda s, kt: s >= cand) + jnp.where(NEG_INF >= cand, n_extra, 0)

        def count16(ref, pred):
            parts = []
            for kt in range(nk):
                hit = jnp.where(pred(ref[_rows(kt), :]), jnp.int16(1), jnp.int16(0))
                parts += [hit[r * 16:(r + 1) * 16] for r in range(TK // 16)]
            while len(parts) > 1:
                odd = parts[-1:] if len(parts) % 2 else []
                parts = [a + b for a, b in zip(parts[0::2], parts[1::2])] + odd
            return jnp.sum(parts[0].astype(I32), axis=0, keepdims=True)

        def kth_largest16(ref, need, extra_ge):
            def bit_body(i, t_u):
                c_u = t_u | jnp.left_shift(jnp.int32(1), 15 - i)
                c_s = c_u - 32768
                c16 = c_s.astype(I16)
                cnt = count16(ref, lambda x: x >= c16) + extra_ge(c_s)
                return jnp.where(cnt >= need, c_u, t_u)
            return lax.fori_loop(0, 16, bit_body, jnp.zeros((1, TQ), I32)) - 32768

        t_hi = kth_largest16(hi_sc, topk, lambda c: jnp.where(neg_hi >= c, n_extra, 0))
        t_hi16 = t_hi.astype(I16)
        above = count16(hi_sc, lambda x: x > t_hi16) + jnp.where(neg_hi > t_hi, n_extra, 0)
        for kt in range(nk):
            lo_sc[_rows(kt), :] = jnp.where(hi_sc[_rows(kt), :] == t_hi16, lo_sc[_rows(kt), :],
                                            jnp.int16(-32768))
        t_lo = kth_largest16(
            lo_sc, topk - above,
            lambda c: jnp.where(jnp.logical_and(t_hi == neg_hi, neg_lo >= c), n_extra, 0))
        key_fast = t_hi * 65536 + (t_lo + 32768)

        cnt_fast = count_ge(_key_to_float(key_fast))
        cnt_next = count_ge(_key_to_float(_next_key_up(key_fast)))
        certified = jnp.min(jnp.where(jnp.logical_and(cnt_fast >= topk, cnt_next < topk), 1, 0)) > 0

        def float_search():
            def bit_body(i, t_u):
                c_u = t_u | jnp.left_shift(jnp.int32(1), 31 - i)
                return jnp.where(count_ge(_key_to_float(c_u ^ INT_MIN)) >= topk, c_u, t_u)
            key = lax.fori_loop(0, 32, bit_body, jnp.zeros((1, TQ), I32)) ^ INT_MIN
            return key, count_ge(_key_to_float(key))

        thr_key, cnt_ge = lax.cond(certified, lambda: (key_fast, cnt_fast), float_search)
        thr = _key_to_float(thr_key)
        has_tie = jnp.max(jnp.where(cnt_ge != topk, 1, 0)) > 0

        def tie_limit():
            cnt_gt = count(lambda s, kt: s > thr) + jnp.where(NEG_INF > thr, n_extra, 0)
            need = topk - cnt_gt

            def idx_body(i, p):
                c = p | jnp.left_shift(jnp.int32(1), 10 - i)
                below = count(lambda s, kt: jnp.logical_and(s == thr, row_iota + kt * TK < c))
                return jnp.where(below < need, c, p)

            return lax.fori_loop(0, 11, idx_body, jnp.zeros((1, TQ), I32))

        p_lim = lax.cond(has_tie, tie_limit, lambda: jnp.full((1, TQ), seq_len, I32))

        qa = qat_ref[0]
        qa_pads = []
        for h in range(HEADS):
            qh = qa[h * 64:(h + 1) * 64]
            qa_pads.append(jnp.concatenate([zeros64, qh] if h % 2 else [qh, zeros64], axis=0))

        def bias_fn(kt):
            s = sc_sc[_rows(kt), :]
            sel = jnp.logical_or(s > thr, jnp.logical_and(s == thr, row_iota + kt * TK <= p_lim))
            sel = jnp.logical_and(sel, _admissible(kt, j))
            return jnp.where(sel, 0.0, -jnp.inf)

        def qk_fn(h, kt):
            kpair = ka_ref[0, _rows(kt), (h // 2) * 128:(h // 2 + 1) * 128]
            return jnp.dot(kpair, qa_pads[h], preferred_element_type=F32)

        m = _logits_pass(nk, qk_fn, bias_fn, s_sc)
        _exp_pv(nk, m, vat_ref, gat_ref, out_ref, s_sc)

    _for_each_tile_count(seq_len, j, run)


def _mla_kernel(seq_len, kb_ref, vbt_ref, qbt_ref, gbt_ref, out_ref, s_sc):
    j = pl.program_id(1)

    def run(nk):
        qb = qbt_ref[0]

        def bias_fn(kt):
            return jnp.where(_admissible(kt, j), 0.0, -jnp.inf) if kt == nk - 1 else None

        def qk_fn(h, kt):
            kh = kb_ref[0, _rows(kt), h * B_PAD:(h + 1) * B_PAD]
            return jnp.dot(kh, qb[h * B_PAD:(h + 1) * B_PAD], preferred_element_type=F32)

        m = _logits_pass(nk, qk_fn, bias_fn, s_sc)
        _exp_pv(nk, m, vbt_ref, gbt_ref, out_ref, s_sc)

    _for_each_tile_count(seq_len, j, run)


def _out_kernel(x_ref, mat_ref, mbt_ref, w_ref, out_ref):
    mixed_t = jnp.concatenate([mat_ref[0], mbt_ref[0]], axis=0)
    y = lax.dot_general(mixed_t, w_ref[...], (((0,), (0,)), ((), ())),
                        preferred_element_type=F32)
    out_ref[0] = x_ref[0] + y


def _rope_tables_t(seq_len, dim):
    half = dim // 2
    freqs = jnp.power(ROPE_THETA, -jnp.arange(half, dtype=F32) * 2.0 / dim)
    ang = jnp.arange(seq_len, dtype=jnp.int32).astype(F32)[None, :] * freqs[:, None]
    return jnp.cos(ang), jnp.sin(ang)


def _params(sem):
    return pltpu.CompilerParams(dimension_semantics=sem, vmem_limit_bytes=V7X_VMEM_LIMIT)


def _layer(h, norm_gain, w_in, a_q_norm, a_k_norm, b_q_latent_norm, b_kv_latent_norm,
           w_uq, w_ukv, b_q_norm, b_k_norm, w_out):
    B, S, D = h.shape
    assert D == D_MODEL and S % TM == 0 and TM % TK == 0 and TK % TQ == 0 and S <= 2048
    nst, nkt, nqb = S // TM, S // TK, S // TQ
    vext = HEADS * V_EXT

    w_t = w_in.T.astype(BF16)
    wuq_t = jnp.pad(w_uq.reshape(Q_LORA, HEADS, B_QK), ((0, 0), (0, 0), (0, B_PAD - B_QK)))
    wuq_t = wuq_t.reshape(Q_LORA, HEADS * B_PAD).T.astype(BF16)
    wukv = w_ukv.reshape(KV_LORA, HEADS, B_NOPE + HEAD_DIM)
    wuk_t = wukv[:, :, :B_NOPE].reshape(KV_LORA, WIDTH).T.astype(BF16)
    wuv_t = wukv[:, :, B_NOPE:].reshape(KV_LORA, WIDTH).T.astype(BF16)
    lanes = lambda g: jnp.broadcast_to(g.astype(F32)[:, None], (g.shape[0], TM))
    pad_b = lambda g: jnp.pad(g, (0, B_PAD - B_QK))
    c64, s64 = _rope_tables_t(S, HEAD_DIM)
    c32, s32 = _rope_tables_t(S, B_ROPE)

    const = lambda shape: pl.BlockSpec(shape, lambda b, i: (0,) * len(shape))
    tok_t = lambda c: pl.BlockSpec((1, c, TM), lambda b, i: (b, 0, i))
    tok = lambda c: pl.BlockSpec((1, TM, c), lambda b, i: (b, i, 0))
    tiles = lambda c: pl.BlockSpec((1, TM // TK, c, TK), lambda b, i: (b, i, 0, 0))
    tab = lambda r: pl.BlockSpec((r, TM), lambda b, i: (0, i))
    sds = jax.ShapeDtypeStruct

    (qat, ka, vat, gat, qit, ki, wit, qbt, kb, vbt, gbt) = pl.pallas_call(
        _proj_kernel,
        grid=(B, nst),
        in_specs=[tok(D), const((1, D)), const((O_END, D)), const((HEADS * B_PAD, Q_LORA)),
                  const((WIDTH, KV_LORA)), const((WIDTH, KV_LORA)),
                  const((64, TM)), const((64, TM)), const((Q_LORA, TM)), const((KV_LORA, TM)),
                  const((B_PAD, TM)), const((B_PAD, TM)),
                  tab(32), tab(32), tab(16), tab(16)],
        out_specs=[tok_t(WIDTH), tok(WIDTH), tiles(vext), tok_t(WIDTH), tok_t(WIDTH), tok(128),
                   tok_t(HEADS), tok_t(HEADS * B_PAD), tok(HEADS * B_PAD), tiles(vext), tok_t(WIDTH)],
        out_shape=[sds((B, WIDTH, S), BF16), sds((B, S, WIDTH), BF16), sds((B, nkt, vext, TK), BF16),
                   sds((B, WIDTH, S), BF16), sds((B, WIDTH, S), BF16), sds((B, S, 128), BF16),
                   sds((B, HEADS, S), F32), sds((B, HEADS * B_PAD, S), BF16),
                   sds((B, S, HEADS * B_PAD), BF16), sds((B, nkt, vext, TK), BF16),
                   sds((B, WIDTH, S), BF16)],
        compiler_params=_params(("arbitrary", "arbitrary")),
        name="proj",
    )(h, norm_gain.reshape(1, D), w_t, wuq_t, wuk_t, wuv_t,
      lanes(a_q_norm), lanes(a_k_norm), lanes(b_q_latent_norm), lanes(b_kv_latent_norm),
      lanes(pad_b(b_q_norm)), lanes(pad_b(b_k_norm)), c64, s64, c32, s32)

    per_b = lambda *shape: pl.BlockSpec((1,) + shape, lambda b, j: (b,) + (0,) * len(shape))
    qblk = lambda c: pl.BlockSpec((1, c, TQ), lambda b, j: (b, 0, j))
    attn_scratch = [pltpu.VMEM((HEADS, S, TQ), F32)]

    mixed_a = pl.pallas_call(
        functools.partial(_dsa_kernel, S),
        grid=(B, nqb),
        in_specs=[per_b(S, 128), per_b(S, WIDTH), per_b(nkt, vext, TK),
                  qblk(WIDTH), qblk(HEADS), qblk(WIDTH), qblk(WIDTH)],
        out_specs=qblk(WIDTH),
        out_shape=sds((B, WIDTH, S), BF16),
        scratch_shapes=[pltpu.VMEM((S, TQ), F32), pltpu.VMEM((S, TQ), I16),
                        pltpu.VMEM((S, TQ), I16)] + attn_scratch,
        compiler_params=_params(("arbitrary", "arbitrary")),
        name="dsa",
    )(ki, ka, vat, qit, wit, qat, gat)

    mixed_b = pl.pallas_call(
        functools.partial(_mla_kernel, S),
        grid=(B, nqb),
        in_specs=[per_b(S, HEADS * B_PAD), per_b(nkt, vext, TK), qblk(HEADS * B_PAD), qblk(WIDTH)],
        out_specs=qblk(WIDTH),
        out_shape=sds((B, WIDTH, S), BF16),
        scratch_shapes=attn_scratch,
        compiler_params=_params(("arbitrary", "arbitrary")),
        name="mla",
    )(kb, vbt, qbt, gbt)

    return pl.pallas_call(
        _out_kernel,
        grid=(B, nst),
        in_specs=[tok(D), tok_t(WIDTH), tok_t(WIDTH), const((2 * WIDTH, D))],
        out_specs=tok(D),
        out_shape=sds((B, S, D), F32),
        compiler_params=_params(("arbitrary", "arbitrary")),
        name="outproj",
    )(h, mixed_a, mixed_b, w_out.astype(BF16))


def kernel(x, norm_gain, w_in, a_q_norm, a_k_norm, b_q_latent_norm, b_kv_latent_norm,
           w_uq, w_ukv, b_q_norm, b_k_norm, w_out):
    h = x
    for l in range(norm_gain.shape[0]):
        h = _layer(h, norm_gain[l], w_in[l], a_q_norm[l], a_k_norm[l], b_q_latent_norm[l],
                   b_kv_latent_norm[l], w_uq[l], w_ukv[l], b_q_norm[l], b_k_norm[l], w_out[l])
    return h
```

```python
import functools

import numpy as np
import jax
import jax.numpy as jnp
from jax import lax
from jax.experimental import pallas as pl
from jax.experimental.pallas import tpu as pltpu

F32 = jnp.float32
BF16 = jnp.bfloat16
I32 = jnp.int32
I16 = jnp.int16

D_MODEL = 1024
CHUNK = 64
ROPE_THETA = 10000.0
RMS_EPS = 1e-6
NEG_INF = -1e30
HEADS = 8
HEAD_DIM = 64
WIDTH = HEADS * HEAD_DIM
IDX_DIM = 64
IDX_ROPE_DIM = 32
TOPK_MAX = 256
B_NOPE = 64
B_ROPE = 32
B_QK = B_NOPE + B_ROPE
B_PAD = 128
Q_LORA = 384
KV_LORA = 256
V_EXT = 80

O_QA, O_KA, O_VA, O_GA = 0, 512, 1024, 1536
O_QI, O_KI, O_WI = 2048, 2560, 2624
O_CQ, O_CKV, O_KR, O_GB, O_END = 2632, 3016, 3272, 3304, 3816

TM = 512
TO = 1024
TQ = 256
TK = 512
V7X_VMEM_LIMIT = 56 * 1024 * 1024

INT_MIN = -(2 ** 31)
F32_LOWEST = float(np.finfo(np.float32).min)
LOG2E = float(np.log2(np.e))
KEY_POS_INF = 0x7F800000
KEY_NEG_INF = INT_MIN + 0x00800000
KEY_MIN_NORMAL = 0x00800000
_NEG_BITS = int(np.array(NEG_INF, np.float32).view(np.int32))
NEG_KEY = int(np.int32(np.int64(INT_MIN) - np.int64(_NEG_BITS)))


def _rope_rows(a, b, cos, sin):
    return a * cos - b * sin, b * cos + a * sin


def _head_norm_rope_t(xh, gain, cos, sin, n_true, rope_lo, half):
    ssq = jnp.sum(xh * xh, axis=0, keepdims=True)
    y = (xh * lax.rsqrt(ssq * (1.0 / n_true) + RMS_EPS)) * gain
    ra, rb = _rope_rows(y[rope_lo:rope_lo + half], y[rope_lo + half:rope_lo + 2 * half], cos, sin)
    parts = []
    if rope_lo:
        parts.append(y[:rope_lo])
    parts += [ra, rb]
    if rope_lo + 2 * half < y.shape[0]:
        parts.append(y[rope_lo + 2 * half:])
    return jnp.concatenate(parts, axis=0)


def _silu(g):
    return g * (1.0 / (1.0 + jnp.exp(-g)))


def _key_to_float(key):
    key = jnp.clip(key, KEY_NEG_INF, KEY_POS_INF)
    key = jnp.where(jnp.abs(key) < KEY_MIN_NORMAL, 0, key)
    return lax.bitcast_convert_type(jnp.where(key < 0, INT_MIN - key, key), F32)


def _next_key_up(key):
    nxt = jnp.minimum(key, KEY_POS_INF) + 1
    return jnp.where(jnp.logical_and(nxt > 0, nxt < KEY_MIN_NORMAL), KEY_MIN_NORMAL, nxt)


def _proj_kernel(x_ref, ng_ref, w_ref, wuq_ref, wuk_ref, wuv_ref,
                 gaq_ref, gak_ref, gql_ref, gkvl_ref, gbq_ref, gbk_ref,
                 c64_ref, s64_ref, c32_ref, s32_ref,
                 qat_ref, ka_ref, vat_ref, gat_ref, qit_ref, ki_ref, wit_ref,
                 qbt_ref, kb_ref, vbt_ref, gbt_ref):
    x = x_ref[0]
    ms = jnp.mean(x * x, axis=-1, keepdims=True)
    xb = ((x * lax.rsqrt(ms + RMS_EPS)) * ng_ref[...]).astype(BF16)

    def proj_t(lo, hi):
        return lax.dot_general(w_ref[lo:hi, :], xb, (((1,), (1,)), ((), ())),
                               preferred_element_type=F32)

    c64, s64 = c64_ref[...], s64_ref[...]
    c32, s32 = c32_ref[...], s32_ref[...]
    nkt = TM // TK
    ones_rows = jnp.ones((V_EXT - HEAD_DIM, TM), F32)

    def store_tiles(ref, val):
        ext = jnp.concatenate(
            [r for h in range(HEADS) for r in (val[h * 64:(h + 1) * 64], ones_rows)], axis=0).astype(BF16)
        for t in range(nkt):
            ref[0, t] = ext[:, t * TK:(t + 1) * TK]

    qk = proj_t(O_QA, O_VA)
    qa = qk[:WIDTH]
    gaq = gaq_ref[...]
    qa = jnp.concatenate(
        [_head_norm_rope_t(qa[h * 64:(h + 1) * 64], gaq, c64, s64, 64, 0, 32) for h in range(HEADS)], axis=0)
    qat_ref[0] = (qa * (HEAD_DIM ** -0.5 * LOG2E)).astype(BF16)

    ka = qk[WIDTH:]
    gak = gak_ref[...]
    ka = jnp.concatenate(
        [_head_norm_rope_t(ka[h * 64:(h + 1) * 64], gak, c64, s64, 64, 0, 32) for h in range(HEADS)], axis=0)
    ka_ref[0] = ka.T.astype(BF16)

    vg = proj_t(O_VA, O_QI)
    store_tiles(vat_ref, vg[:WIDTH])
    gat_ref[0] = _silu(vg[WIDTH:]).astype(BF16)

    mid = proj_t(O_QI, O_GB)
    qi = mid[:O_KI - O_QI]
    qi_parts = []
    for h in range(HEADS):
        qh = qi[h * 64:(h + 1) * 64]
        ra, rb = _rope_rows(qh[0:16], qh[16:32], c32, s32)
        qi_parts += [ra, rb, qh[32:]]
    qit_ref[0] = jnp.concatenate(qi_parts, axis=0).astype(BF16)

    kw = mid[O_KI - O_QI:O_CQ - O_QI]
    ra, rb = _rope_rows(kw[0:16], kw[16:32], c32, s32)
    ki = jnp.concatenate([ra, rb, kw[32:64], jnp.zeros((64, TM), F32)], axis=0)
    ki_ref[0] = ki.T.astype(BF16)
    wit_ref[0] = kw[64:72]

    lat = mid[O_CQ - O_QI:]
    cq = lat[:Q_LORA]
    cq = (cq * lax.rsqrt(jnp.mean(cq * cq, axis=0, keepdims=True) + RMS_EPS)) * gql_ref[...]
    qb = jnp.dot(wuq_ref[...], cq.astype(BF16), preferred_element_type=F32)
    gbq = gbq_ref[...]
    qb = jnp.concatenate(
        [_head_norm_rope_t(qb[h * B_PAD:(h + 1) * B_PAD], gbq, c32, s32, B_QK, B_NOPE, 16)
         for h in range(HEADS)], axis=0)
    qbt_ref[0] = (qb * (B_QK ** -0.5 * LOG2E)).astype(BF16)

    ckv = lat[Q_LORA:Q_LORA + KV_LORA]
    ckv = ((ckv * lax.rsqrt(jnp.mean(ckv * ckv, axis=0, keepdims=True) + RMS_EPS)) * gkvl_ref[...]).astype(BF16)
    kr = lat[Q_LORA + KV_LORA:]
    kn = jnp.dot(wuk_ref[...], ckv, preferred_element_type=F32)
    gbk = gbk_ref[...]
    zpad = jnp.zeros((B_PAD - B_QK, TM), F32)
    kb = jnp.concatenate(
        [_head_norm_rope_t(jnp.concatenate([kn[h * 64:(h + 1) * 64], kr, zpad], axis=0),
                           gbk, c32, s32, B_QK, B_NOPE, 16) for h in range(HEADS)], axis=0)
    kb_ref[0] = kb.T.astype(BF16)
    vb = jnp.dot(wuv_ref[...], ckv, preferred_element_type=F32)
    store_tiles(vbt_ref, vb)

    gbt_ref[0] = _silu(proj_t(O_GB, O_END)).astype(BF16)


def _rows(kt):
    return slice(kt * TK, (kt + 1) * TK)


def _logits_pass(nk, qk_fn, bias_fn, s_sc):
    mrun = [jnp.full((8, TQ), F32_LOWEST, F32) for _ in range(HEADS)]
    for kt in range(nk):
        bias = bias_fn(kt)
        for h in range(HEADS):
            s = qk_fn(h, kt)
            if bias is not None:
                s = s + bias
            s_sc[h, _rows(kt), :] = s
            mrun[h] = jnp.maximum(mrun[h], jnp.max(s.reshape(TK // 8, 8, TQ), axis=0))
    return [jnp.max(mrun[h], axis=0, keepdims=True) for h in range(HEADS)]


def _exp_pv(nk, m, vt_ref, gate_ref, out_ref, s_sc):
    for h in range(HEADS):
        acc = jnp.zeros((V_EXT, TQ), F32)
        for kt in range(nk):
            p = jnp.exp2((s_sc[h, _rows(kt), :] - m[h]).astype(BF16))
            acc = acc + jnp.dot(vt_ref[0, kt, h * V_EXT:(h + 1) * V_EXT, :], p,
                                preferred_element_type=F32)
        o = acc[:HEAD_DIM] * (1.0 / acc[HEAD_DIM:HEAD_DIM + 1])
        g = gate_ref[0, h * 64:(h + 1) * 64, :].astype(F32)
        out_ref[0, h * 64:(h + 1) * 64, :] = (o * g).astype(BF16)


def _num_key_tiles(j):
    return (j * TQ) // TK + 1


def _for_each_tile_count(seq_len, j, run):
    nk = _num_key_tiles(j)
    for n in range(1, seq_len // TK + 1):
        pl.when(nk == n)(functools.partial(run, n))


def _admissible(kt, j):
    key = lax.broadcasted_iota(I32, (TK, TQ), 0) + kt * TK
    qpos = lax.broadcasted_iota(I32, (1, TQ), 1) + j * TQ
    return key < ((qpos >> 6) + 1) * CHUNK


def _dsa_kernel(seq_len, ki_ref, ka_ref, vat_ref, qit_ref, wit_ref, qat_ref, gat_ref, out_ref,
                sc_sc, hi_sc, lo_sc, s_sc):
    j = pl.program_id(1)
    topk = min(TOPK_MAX, seq_len // 4)
    idx_scale = (IDX_DIM * HEADS) ** -0.5
    neg_hi, neg_lo = NEG_KEY >> 16, (NEG_KEY & 0xFFFF) - 32768

    def run(nk):
        n_extra = seq_len - nk * TK
        idx_bits = (nk * TK - 1).bit_length()
        zeros64 = jnp.zeros((64, TQ), BF16)
        row_iota = lax.broadcasted_iota(I32, (TK, TQ), 0)

        qi = qit_ref[0]
        w = wit_ref[0]
        qpads = [jnp.concatenate([qi[h * 64:(h + 1) * 64], zeros64], axis=0) for h in range(HEADS)]
        for kt in range(nk):
            kt_tile = ki_ref[0, _rows(kt), :]
            acc = jnp.zeros((TK, TQ), F32)
            for h in range(HEADS):
                rel = jnp.dot(kt_tile, qpads[h], preferred_element_type=F32)
                acc = acc + jnp.maximum(rel, 0.0) * w[h:h + 1, :]
            score = acc * idx_scale
            if kt == nk - 1:
                score = jnp.where(_admissible(kt, j), score, NEG_INF)
            sc_sc[_rows(kt), :] = score
            bits = lax.bitcast_convert_type(score, I32)
            key = jnp.where(bits < 0, INT_MIN - bits, bits)
            hi_sc[_rows(kt), :] = (key >> 16).astype(I16)
            lo_sc[_rows(kt), :] = ((key & 0xFFFF) - 32768).astype(I16)

        def count(pred):
            part = jnp.zeros((8, TQ), I32)
            for kt in range(nk):
                hit = pred(sc_sc[_rows(kt), :], kt).astype(I32)
                part = part + jnp.sum(hit.reshape(TK // 8, 8, TQ), axis=0)
            return jnp.sum(part, axis=0, keepdims=True)

        def count_ge(cand):
            return count(lambda s, kt: s >= cand) + jnp.where(NEG_INF >= cand, n_extra, 0)

        def count16(ref, pred):
            parts = []
            for kt in range(nk):
                hit = jnp.where(pred(ref[_rows(kt), :]), jnp.int16(1), jnp.int16(0))
                parts += [hit[r * 16:(r + 1) * 16] for r in range(TK // 16)]
            while len(parts) > 1:
                odd = parts[-1:] if len(parts) % 2 else []
                parts = [a + b for a, b in zip(parts[0::2], parts[1::2])] + odd
            return jnp.sum(parts[0].astype(I32), axis=0, keepdims=True)

        def kth_largest16(ref, need, extra_ge):
            def bit_body(i, t_u):
                c_u = t_u | jnp.left_shift(jnp.int32(1), 15 - i)
                c_s = c_u - 32768
                c16 = c_s.astype(I16)
                cnt = count16(ref, lambda x: x >= c16) + extra_ge(c_s)
                return jnp.where(cnt >= need, c_u, t_u)
            return lax.fori_loop(0, 16, bit_body, jnp.zeros((1, TQ), I32)) - 32768

        t_hi = kth_largest16(hi_sc, topk, lambda c: jnp.where(neg_hi >= c, n_extra, 0))
        t_hi16 = t_hi.astype(I16)
        above = count16(hi_sc, lambda x: x > t_hi16) + jnp.where(neg_hi > t_hi, n_extra, 0)
        for kt in range(nk):
            lo_sc[_rows(kt), :] = jnp.where(hi_sc[_rows(kt), :] == t_hi16, lo_sc[_rows(kt), :],
                                            jnp.int16(-32768))
        t_lo = kth_largest16(
            lo_sc, topk - above,
            lambda c: jnp.where(jnp.logical_and(t_hi == neg_hi, neg_lo >= c), n_extra, 0))
        key_fast = t_hi * 65536 + (t_lo + 32768)

        cnt_fast = count_ge(_key_to_float(key_fast))
        cnt_next = count_ge(_key_to_float(_next_key_up(key_fast)))
        certified = jnp.min(jnp.where(jnp.logical_and(cnt_fast >= topk, cnt_next < topk), 1, 0)) > 0

        def float_search():
            def bit_body(i, t_u):
                c_u = t_u | jnp.left_shift(jnp.int32(1), 31 - i)
                return jnp.where(count_ge(_key_to_float(c_u ^ INT_MIN)) >= topk, c_u, t_u)
            key = lax.fori_loop(0, 32, bit_body, jnp.zeros((1, TQ), I32)) ^ INT_MIN
            return key, count_ge(_key_to_float(key))

        thr_key, cnt_ge = lax.cond(certified, lambda: (key_fast, cnt_fast), float_search)
        thr = _key_to_float(thr_key)
        has_tie = jnp.max(jnp.where(cnt_ge != topk, 1, 0)) > 0

        def tie_limit():
            cnt_gt = count(lambda s, kt: s > thr) + jnp.where(NEG_INF > thr, n_extra, 0)
            need = topk - cnt_gt

            def idx_body(i, p):
                c = p | jnp.left_shift(jnp.int32(1), idx_bits - 1 - i)
                below = count(lambda s, kt: jnp.logical_and(s == thr, row_iota + kt * TK < c))
                return jnp.where(below < need, c, p)

            return lax.fori_loop(0, idx_bits, idx_body, jnp.zeros((1, TQ), I32))

        p_lim = lax.cond(has_tie, tie_limit, lambda: jnp.full((1, TQ), seq_len, I32))

        qa = qat_ref[0]
        qa_pads = []
        for h in range(HEADS):
            qh = qa[h * 64:(h + 1) * 64]
            qa_pads.append(jnp.concatenate([zeros64, qh] if h % 2 else [qh, zeros64], axis=0))

        def bias_fn(kt):
            s = sc_sc[_rows(kt), :]
            sel = jnp.logical_or(s > thr, jnp.logical_and(s == thr, row_iota + kt * TK <= p_lim))
            if kt == nk - 1:
                sel = jnp.logical_and(sel, _admissible(kt, j))
            return jnp.where(sel, 0.0, -jnp.inf)

        def qk_fn(h, kt):
            kpair = ka_ref[0, _rows(kt), (h // 2) * 128:(h // 2 + 1) * 128]
            return jnp.dot(kpair, qa_pads[h], preferred_element_type=F32)

        m = _logits_pass(nk, qk_fn, bias_fn, s_sc)
        _exp_pv(nk, m, vat_ref, gat_ref, out_ref, s_sc)

    _for_each_tile_count(seq_len, j, run)


def _mla_kernel(seq_len, kb_ref, vbt_ref, qbt_ref, gbt_ref, out_ref, s_sc):
    j = pl.program_id(1)

    def run(nk):
        qb = qbt_ref[0]

        def bias_fn(kt):
            return jnp.where(_admissible(kt, j), 0.0, -jnp.inf) if kt == nk - 1 else None

        def qk_fn(h, kt):
            kh = kb_ref[0, _rows(kt), h * B_PAD:(h + 1) * B_PAD]
            return jnp.dot(kh, qb[h * B_PAD:(h + 1) * B_PAD], preferred_element_type=F32)

        m = _logits_pass(nk, qk_fn, bias_fn, s_sc)
        _exp_pv(nk, m, vbt_ref, gbt_ref, out_ref, s_sc)

    _for_each_tile_count(seq_len, j, run)


def _out_kernel(x_ref, mat_ref, mbt_ref, w_ref, out_ref):
    mixed_t = jnp.concatenate([mat_ref[0], mbt_ref[0]], axis=0)
    y = lax.dot_general(mixed_t, w_ref[...], (((0,), (0,)), ((), ())),
                        preferred_element_type=F32)
    out_ref[0] = x_ref[0] + y


def _rope_tables_t(seq_len, dim):
    half = dim // 2
    freqs = jnp.power(ROPE_THETA, -jnp.arange(half, dtype=F32) * 2.0 / dim)
    ang = jnp.arange(seq_len, dtype=jnp.int32).astype(F32)[None, :] * freqs[:, None]
    return jnp.cos(ang), jnp.sin(ang)


def _params(sem):
    return pltpu.CompilerParams(dimension_semantics=sem, vmem_limit_bytes=V7X_VMEM_LIMIT)


def _layer(h, norm_gain, w_in, a_q_norm, a_k_norm, b_q_latent_norm, b_kv_latent_norm,
           w_uq, w_ukv, b_q_norm, b_k_norm, w_out):
    B, S, D = h.shape
    assert D == D_MODEL and S % TM == 0 and S % TO == 0 and TM % TK == 0 and TK % TQ == 0 and S <= 2048
    nst, nkt, nqb = S // TM, S // TK, S // TQ
    vext = HEADS * V_EXT

    w_t = w_in.T.astype(BF16)
    wuq_t = jnp.pad(w_uq.reshape(Q_LORA, HEADS, B_QK), ((0, 0), (0, 0), (0, B_PAD - B_QK)))
    wuq_t = wuq_t.reshape(Q_LORA, HEADS * B_PAD).T.astype(BF16)
    wukv = w_ukv.reshape(KV_LORA, HEADS, B_NOPE + HEAD_DIM)
    wuk_t = wukv[:, :, :B_NOPE].reshape(KV_LORA, WIDTH).T.astype(BF16)
    wuv_t = wukv[:, :, B_NOPE:].reshape(KV_LORA, WIDTH).T.astype(BF16)
    lanes = lambda g: jnp.broadcast_to(g.astype(F32)[:, None], (g.shape[0], TM))
    pad_b = lambda g: jnp.pad(g, (0, B_PAD - B_QK))
    c64, s64 = _rope_tables_t(S, HEAD_DIM)
    c32, s32 = _rope_tables_t(S, B_ROPE)

    const = lambda shape: pl.BlockSpec(shape, lambda b, i: (0,) * len(shape))
    tok_t = lambda c: pl.BlockSpec((1, c, TM), lambda b, i: (b, 0, i))
    tok = lambda c: pl.BlockSpec((1, TM, c), lambda b, i: (b, i, 0))
    tiles = lambda c: pl.BlockSpec((1, TM // TK, c, TK), lambda b, i: (b, i, 0, 0))
    tab = lambda r: pl.BlockSpec((r, TM), lambda b, i: (0, i))
    sds = jax.ShapeDtypeStruct

    (qat, ka, vat, gat, qit, ki, wit, qbt, kb, vbt, gbt) = pl.pallas_call(
        _proj_kernel,
        grid=(B, nst),
        in_specs=[tok(D), const((1, D)), const((O_END, D)), const((HEADS * B_PAD, Q_LORA)),
                  const((WIDTH, KV_LORA)), const((WIDTH, KV_LORA)),
                  const((64, TM)), const((64, TM)), const((Q_LORA, TM)), const((KV_LORA, TM)),
                  const((B_PAD, TM)), const((B_PAD, TM)),
                  tab(32), tab(32), tab(16), tab(16)],
        out_specs=[tok_t(WIDTH), tok(WIDTH), tiles(vext), tok_t(WIDTH), tok_t(WIDTH), tok(128),
                   tok_t(HEADS), tok_t(HEADS * B_PAD), tok(HEADS * B_PAD), tiles(vext), tok_t(WIDTH)],
        out_shape=[sds((B, WIDTH, S), BF16), sds((B, S, WIDTH), BF16), sds((B, nkt, vext, TK), BF16),
                   sds((B, WIDTH, S), BF16), sds((B, WIDTH, S), BF16), sds((B, S, 128), BF16),
                   sds((B, HEADS, S), F32), sds((B, HEADS * B_PAD, S), BF16),
                   sds((B, S, HEADS * B_PAD), BF16), sds((B, nkt, vext, TK), BF16),
                   sds((B, WIDTH, S), BF16)],
        compiler_params=_params(("arbitrary", "arbitrary")),
        name="proj",
    )(h, norm_gain.reshape(1, D), w_t, wuq_t, wuk_t, wuv_t,
      lanes(a_q_norm), lanes(a_k_norm), lanes(b_q_latent_norm), lanes(b_kv_latent_norm),
      lanes(pad_b(b_q_norm)), lanes(pad_b(b_k_norm)), c64, s64, c32, s32)

    per_b = lambda *shape: pl.BlockSpec((1,) + shape, lambda b, j: (b,) + (0,) * len(shape))
    qblk = lambda c: pl.BlockSpec((1, c, TQ), lambda b, j: (b, 0, j))
    attn_scratch = [pltpu.VMEM((HEADS, S, TQ), F32)]

    mixed_a = pl.pallas_call(
        functools.partial(_dsa_kernel, S),
        grid=(B, nqb),
        in_specs=[per_b(S, 128), per_b(S, WIDTH), per_b(nkt, vext, TK),
                  qblk(WIDTH), qblk(HEADS), qblk(WIDTH), qblk(WIDTH)],
        out_specs=qblk(WIDTH),
        out_shape=sds((B, WIDTH, S), BF16),
        scratch_shapes=[pltpu.VMEM((S, TQ), F32), pltpu.VMEM((S, TQ), I16),
                        pltpu.VMEM((S, TQ), I16)] + attn_scratch,
        compiler_params=_params(("arbitrary", "arbitrary")),
        name="dsa",
    )(ki, ka, vat, qit, wit, qat, gat)

    mixed_b = pl.pallas_call(
        functools.partial(_mla_kernel, S),
        grid=(B, nqb),
        in_specs=[per_b(S, HEADS * B_PAD), per_b(nkt, vext, TK), qblk(HEADS * B_PAD), qblk(WIDTH)],
        out_specs=qblk(WIDTH),
        out_shape=sds((B, WIDTH, S), BF16),
        scratch_shapes=attn_scratch,
        compiler_params=_params(("arbitrary", "arbitrary")),
        name="mla",
    )(kb, vbt, qbt, gbt)

    return pl.pallas_call(
        _out_kernel,
        grid=(B, S // TO),
        in_specs=[pl.BlockSpec((1, TO, D), lambda b, i: (b, i, 0)),
                  pl.BlockSpec((1, WIDTH, TO), lambda b, i: (b, 0, i)),
                  pl.BlockSpec((1, WIDTH, TO), lambda b, i: (b, 0, i)), const((2 * WIDTH, D))],
        out_specs=pl.BlockSpec((1, TO, D), lambda b, i: (b, i, 0)),
        out_shape=sds((B, S, D), F32),
        compiler_params=_params(("arbitrary", "arbitrary")),
        name="outproj",
    )(h, mixed_a, mixed_b, w_out.astype(BF16))


def kernel(x, norm_gain, w_in, a_q_norm, a_k_norm, b_q_latent_norm, b_kv_latent_norm,
           w_uq, w_ukv, b_q_norm, b_k_norm, w_out):
    h = x
    for l in range(norm_gain.shape[0]):
        h = _layer(h, norm_gain[l], w_in[l], a_q_norm[l], a_k_norm[l], b_q_latent_norm[l],
                   b_kv_latent_norm[l], w_uq[l], w_ukv[l], b_q_norm[l], b_k_norm[l], w_out[l])
    return h
```

```python
import functools

import numpy as np
import jax
import jax.numpy as jnp
from jax import lax
from jax.experimental import pallas as pl
from jax.experimental.pallas import tpu as pltpu

F32 = jnp.float32
BF16 = jnp.bfloat16
I32 = jnp.int32
I16 = jnp.int16

D_MODEL = 1024
CHUNK = 64
ROPE_THETA = 10000.0
RMS_EPS = 1e-6
NEG_INF = -1e30
HEADS = 8
HEAD_DIM = 64
WIDTH = HEADS * HEAD_DIM
IDX_DIM = 64
IDX_ROPE_DIM = 32
TOPK_MAX = 256
B_NOPE = 64
B_ROPE = 32
B_QK = B_NOPE + B_ROPE
B_PAD = 128
Q_LORA = 384
KV_LORA = 256
V_EXT = 80

O_QA, O_KA, O_VA, O_GA = 0, 512, 1024, 1536
O_QI, O_KI, O_WI = 2048, 2560, 2624
O_CQ, O_CKV, O_KR, O_GB, O_END = 2632, 3016, 3272, 3304, 3816

TM = 512
TO = 1024
TQ = 256
TK = 512
V7X_VMEM_LIMIT = 56 * 1024 * 1024

INT_MIN = -(2 ** 31)
F32_LOWEST = float(np.finfo(np.float32).min)
LOG2E = float(np.log2(np.e))
KEY_POS_INF = 0x7F800000
KEY_NEG_INF = INT_MIN + 0x00800000
KEY_MIN_NORMAL = 0x00800000
_NEG_BITS = int(np.array(NEG_INF, np.float32).view(np.int32))
NEG_KEY = int(np.int32(np.int64(INT_MIN) - np.int64(_NEG_BITS)))


def _rope_rows(a, b, cos, sin):
    return a * cos - b * sin, b * cos + a * sin


def _head_norm_rope_t(xh, gain, cos, sin, n_true, rope_lo, half):
    ssq = jnp.sum(xh * xh, axis=0, keepdims=True)
    y = (xh * lax.rsqrt(ssq * (1.0 / n_true) + RMS_EPS)) * gain
    ra, rb = _rope_rows(y[rope_lo:rope_lo + half], y[rope_lo + half:rope_lo + 2 * half], cos, sin)
    parts = []
    if rope_lo:
        parts.append(y[:rope_lo])
    parts += [ra, rb]
    if rope_lo + 2 * half < y.shape[0]:
        parts.append(y[rope_lo + 2 * half:])
    return jnp.concatenate(parts, axis=0)


def _silu(g):
    return g * (1.0 / (1.0 + jnp.exp(-g)))


def _key_to_float(key):
    key = jnp.clip(key, KEY_NEG_INF, KEY_POS_INF)
    key = jnp.where(jnp.abs(key) < KEY_MIN_NORMAL, 0, key)
    return lax.bitcast_convert_type(jnp.where(key < 0, INT_MIN - key, key), F32)


def _next_key_up(key):
    nxt = jnp.minimum(key, KEY_POS_INF) + 1
    return jnp.where(jnp.logical_and(nxt > 0, nxt < KEY_MIN_NORMAL), KEY_MIN_NORMAL, nxt)


def _proj_kernel(x_ref, ng_ref, w_ref, wuq_ref, wuk_ref, wuv_ref,
                 gaq_ref, gak_ref, gql_ref, gkvl_ref, gbq_ref, gbk_ref,
                 c64_ref, s64_ref, c32_ref, s32_ref,
                 qat_ref, ka_ref, vat_ref, gat_ref, qit_ref, ki_ref, wit_ref,
                 qbt_ref, kb_ref, vbt_ref, gbt_ref):
    x = x_ref[0]
    ms = jnp.mean(x * x, axis=-1, keepdims=True)
    xb = ((x * lax.rsqrt(ms + RMS_EPS)) * ng_ref[...]).astype(BF16)

    def proj_t(lo, hi):
        return lax.dot_general(w_ref[lo:hi, :], xb, (((1,), (1,)), ((), ())),
                               preferred_element_type=F32)

    c64, s64 = c64_ref[...], s64_ref[...]
    c32, s32 = c32_ref[...], s32_ref[...]
    nkt = TM // TK
    ones_rows = jnp.ones((V_EXT - HEAD_DIM, TM), F32)

    def store_tiles(ref, val):
        ext = jnp.concatenate(
            [r for h in range(HEADS) for r in (val[h * 64:(h + 1) * 64], ones_rows)], axis=0).astype(BF16)
        for t in range(nkt):
            ref[0, t] = ext[:, t * TK:(t + 1) * TK]

    qk = proj_t(O_QA, O_VA)
    qa = qk[:WIDTH]
    gaq = gaq_ref[...]
    qa = jnp.concatenate(
        [_head_norm_rope_t(qa[h * 64:(h + 1) * 64], gaq, c64, s64, 64, 0, 32) for h in range(HEADS)], axis=0)
    qat_ref[0] = (qa * (HEAD_DIM ** -0.5 * LOG2E)).astype(BF16)

    ka = qk[WIDTH:]
    gak = gak_ref[...]
    ka = jnp.concatenate(
        [_head_norm_rope_t(ka[h * 64:(h + 1) * 64], gak, c64, s64, 64, 0, 32) for h in range(HEADS)], axis=0)
    ka_ref[0] = ka.T.astype(BF16)

    vg = proj_t(O_VA, O_QI)
    store_tiles(vat_ref, vg[:WIDTH])
    gat_ref[0] = _silu(vg[WIDTH:]).astype(BF16)

    mid = proj_t(O_QI, O_GB)
    qi = mid[:O_KI - O_QI]
    qi_parts = []
    for h in range(HEADS):
        qh = qi[h * 64:(h + 1) * 64]
        ra, rb = _rope_rows(qh[0:16], qh[16:32], c32, s32)
        qi_parts += [ra, rb, qh[32:]]
    qit_ref[0] = jnp.concatenate(qi_parts, axis=0).astype(BF16)

    kw = mid[O_KI - O_QI:O_CQ - O_QI]
    ra, rb = _rope_rows(kw[0:16], kw[16:32], c32, s32)
    ki = jnp.concatenate([ra, rb, kw[32:64], jnp.zeros((64, TM), F32)], axis=0)
    ki_ref[0] = ki.T.astype(BF16)
    wit_ref[0] = kw[64:72]

    lat = mid[O_CQ - O_QI:]
    cq = lat[:Q_LORA]
    cq = (cq * lax.rsqrt(jnp.mean(cq * cq, axis=0, keepdims=True) + RMS_EPS)) * gql_ref[...]
    qb = jnp.dot(wuq_ref[...], cq.astype(BF16), preferred_element_type=F32)
    gbq = gbq_ref[...]
    qb = jnp.concatenate(
        [_head_norm_rope_t(qb[h * B_PAD:(h + 1) * B_PAD], gbq, c32, s32, B_QK, B_NOPE, 16)
         for h in range(HEADS)], axis=0)
    qbt_ref[0] = (qb * (B_QK ** -0.5 * LOG2E)).astype(BF16)

    ckv = lat[Q_LORA:Q_LORA + KV_LORA]
    ckv = ((ckv * lax.rsqrt(jnp.mean(ckv * ckv, axis=0, keepdims=True) + RMS_EPS)) * gkvl_ref[...]).astype(BF16)
    kr = lat[Q_LORA + KV_LORA:]
    kn = jnp.dot(wuk_ref[...], ckv, preferred_element_type=F32)
    gbk = gbk_ref[...]
    zpad = jnp.zeros((B_PAD - B_QK, TM), F32)
    kb = jnp.concatenate(
        [_head_norm_rope_t(jnp.concatenate([kn[h * 64:(h + 1) * 64], kr, zpad], axis=0),
                           gbk, c32, s32, B_QK, B_NOPE, 16) for h in range(HEADS)], axis=0)
    kb_ref[0] = kb.T.astype(BF16)
    vb = jnp.dot(wuv_ref[...], ckv, preferred_element_type=F32)
    store_tiles(vbt_ref, vb)

    gbt_ref[0] = _silu(proj_t(O_GB, O_END)).astype(BF16)


def _key_tiles(jq):
    n_keys = (jq + 1) * TQ
    tiles = [(t * TK, TK) for t in range(n_keys // TK)]
    if n_keys % TK:
        tiles.append((n_keys - n_keys % TK, n_keys % TK))
    return tiles


def _rows(tile):
    return slice(tile[0], tile[0] + tile[1])


def _logits_pass(tiles, qk_fn, bias_fn, s_sc):
    mrun = [jnp.full((8, TQ), F32_LOWEST, F32) for _ in range(HEADS)]
    for tile in tiles:
        bias = bias_fn(tile)
        for h in range(HEADS):
            s = qk_fn(h, tile)
            if bias is not None:
                s = s + bias
            s_sc[h, _rows(tile), :] = s
            mrun[h] = jnp.maximum(mrun[h], jnp.max(s.reshape(tile[1] // 8, 8, TQ), axis=0))
    return [jnp.max(mrun[h], axis=0, keepdims=True) for h in range(HEADS)]


def _exp_pv(tiles, m, vt_ref, gate_ref, out_ref, s_sc):
    for h in range(HEADS):
        acc = jnp.zeros((V_EXT, TQ), F32)
        for start, size in tiles:
            p = jnp.exp2((s_sc[h, start:start + size, :] - m[h]).astype(BF16))
            acc = acc + jnp.dot(vt_ref[0, start // TK, h * V_EXT:(h + 1) * V_EXT, :size], p,
                                preferred_element_type=F32)
        o = acc[:HEAD_DIM] * (1.0 / acc[HEAD_DIM:HEAD_DIM + 1])
        g = gate_ref[0, h * 64:(h + 1) * 64, :].astype(F32)
        out_ref[0, h * 64:(h + 1) * 64, :] = (o * g).astype(BF16)


def _for_each_query_block(seq_len, run):
    j = pl.program_id(1)
    for jq in range(seq_len // TQ):
        pl.when(j == jq)(functools.partial(run, jq))


def _admissible(tile, jq):
    key = lax.broadcasted_iota(I32, (tile[1], TQ), 0) + tile[0]
    qpos = lax.broadcasted_iota(I32, (1, TQ), 1) + jq * TQ
    return key < ((qpos >> 6) + 1) * CHUNK


def _dsa_kernel(seq_len, ki_ref, ka_ref, vat_ref, qit_ref, wit_ref, qat_ref, gat_ref, out_ref,
                sc_sc, hi_sc, lo_sc, s_sc):
    topk = min(TOPK_MAX, seq_len // 4)
    idx_scale = (IDX_DIM * HEADS) ** -0.5
    neg_hi, neg_lo = NEG_KEY >> 16, (NEG_KEY & 0xFFFF) - 32768

    def run(jq):
        tiles = _key_tiles(jq)
        n_keys = (jq + 1) * TQ
        n_extra = seq_len - n_keys
        idx_bits = (n_keys - 1).bit_length()
        zeros64 = jnp.zeros((64, TQ), BF16)

        def key_index(tile):
            return lax.broadcasted_iota(I32, (tile[1], TQ), 0) + tile[0]

        qi = qit_ref[0]
        w = wit_ref[0]
        qpads = [jnp.concatenate([qi[h * 64:(h + 1) * 64], zeros64], axis=0) for h in range(HEADS)]
        for tile in tiles:
            kt_tile = ki_ref[0, _rows(tile), :]
            acc = jnp.zeros((tile[1], TQ), F32)
            for h in range(HEADS):
                rel = jnp.dot(kt_tile, qpads[h], preferred_element_type=F32)
                acc = acc + jnp.maximum(rel, 0.0) * w[h:h + 1, :]
            score = acc * idx_scale
            if tile is tiles[-1]:
                score = jnp.where(_admissible(tile, jq), score, NEG_INF)
            sc_sc[_rows(tile), :] = score
            bits = lax.bitcast_convert_type(score, I32)
            key = jnp.where(bits < 0, INT_MIN - bits, bits)
            hi_sc[_rows(tile), :] = (key >> 16).astype(I16)
            lo_sc[_rows(tile), :] = ((key & 0xFFFF) - 32768).astype(I16)

        def count(pred):
            part = jnp.zeros((8, TQ), I32)
            for tile in tiles:
                hit = pred(sc_sc[_rows(tile), :], tile).astype(I32)
                part = part + jnp.sum(hit.reshape(tile[1] // 8, 8, TQ), axis=0)
            return jnp.sum(part, axis=0, keepdims=True)

        def count_ge(cand):
            return count(lambda s, tile: s >= cand) + jnp.where(NEG_INF >= cand, n_extra, 0)

        def count16(ref, pred):
            parts = []
            for tile in tiles:
                hit = jnp.where(pred(ref[_rows(tile), :]), jnp.int16(1), jnp.int16(0))
                parts += [hit[r * 16:(r + 1) * 16] for r in range(tile[1] // 16)]
            while len(parts) > 1:
                odd = parts[-1:] if len(parts) % 2 else []
                parts = [a + b for a, b in zip(parts[0::2], parts[1::2])] + odd
            return jnp.sum(parts[0].astype(I32), axis=0, keepdims=True)

        def kth_largest16(ref, need, extra_ge):
            def bit_body(i, t_u):
                c_u = t_u | jnp.left_shift(jnp.int32(1), 15 - i)
                c_s = c_u - 32768
                c16 = c_s.astype(I16)
                cnt = count16(ref, lambda x: x >= c16) + extra_ge(c_s)
                return jnp.where(cnt >= need, c_u, t_u)
            return lax.fori_loop(0, 16, bit_body, jnp.zeros((1, TQ), I32)) - 32768

        t_hi = kth_largest16(hi_sc, topk, lambda c: jnp.where(neg_hi >= c, n_extra, 0))
        t_hi16 = t_hi.astype(I16)
        above = count16(hi_sc, lambda x: x > t_hi16) + jnp.where(neg_hi > t_hi, n_extra, 0)
        for tile in tiles:
            lo_sc[_rows(tile), :] = jnp.where(hi_sc[_rows(tile), :] == t_hi16, lo_sc[_rows(tile), :],
                                              jnp.int16(-32768))
        t_lo = kth_largest16(
            lo_sc, topk - above,
            lambda c: jnp.where(jnp.logical_and(t_hi == neg_hi, neg_lo >= c), n_extra, 0))
        key_fast = t_hi * 65536 + (t_lo + 32768)

        cnt_fast = count_ge(_key_to_float(key_fast))
        cnt_next = count_ge(_key_to_float(_next_key_up(key_fast)))
        certified = jnp.min(jnp.where(jnp.logical_and(cnt_fast >= topk, cnt_next < topk), 1, 0)) > 0

        def float_search():
            def bit_body(i, t_u):
                c_u = t_u | jnp.left_shift(jnp.int32(1), 31 - i)
                return jnp.where(count_ge(_key_to_float(c_u ^ INT_MIN)) >= topk, c_u, t_u)
            key = lax.fori_loop(0, 32, bit_body, jnp.zeros((1, TQ), I32)) ^ INT_MIN
            return key, count_ge(_key_to_float(key))

        thr_key, cnt_ge = lax.cond(certified, lambda: (key_fast, cnt_fast), float_search)
        thr = _key_to_float(thr_key)
        has_tie = jnp.max(jnp.where(cnt_ge != topk, 1, 0)) > 0

        def tie_limit():
            cnt_gt = count(lambda s, tile: s > thr) + jnp.where(NEG_INF > thr, n_extra, 0)
            need = topk - cnt_gt

            def idx_body(i, p):
                c = p | jnp.left_shift(jnp.int32(1), idx_bits - 1 - i)
                below = count(lambda s, tile: jnp.logical_and(s == thr, key_index(tile) < c))
                return jnp.where(below < need, c, p)

            return lax.fori_loop(0, idx_bits, idx_body, jnp.zeros((1, TQ), I32))

        p_lim = lax.cond(has_tie, tie_limit, lambda: jnp.full((1, TQ), seq_len, I32))

        qa = qat_ref[0]
        qa_pads = []
        for h in range(HEADS):
            qh = qa[h * 64:(h + 1) * 64]
            qa_pads.append(jnp.concatenate([zeros64, qh] if h % 2 else [qh, zeros64], axis=0))

        def bias_fn(tile):
            s = sc_sc[_rows(tile), :]
            sel = jnp.logical_or(s > thr, jnp.logical_and(s == thr, key_index(tile) <= p_lim))
            if tile is tiles[-1]:
                sel = jnp.logical_and(sel, _admissible(tile, jq))
            return jnp.where(sel, 0.0, -jnp.inf)

        def qk_fn(h, tile):
            kpair = ka_ref[0, _rows(tile), (h // 2) * 128:(h // 2 + 1) * 128]
            return jnp.dot(kpair, qa_pads[h], preferred_element_type=F32)

        m = _logits_pass(tiles, qk_fn, bias_fn, s_sc)
        _exp_pv(tiles, m, vat_ref, gat_ref, out_ref, s_sc)

    _for_each_query_block(seq_len, run)


def _mla_kernel(seq_len, kb_ref, vbt_ref, qbt_ref, gbt_ref, out_ref, s_sc):
    def run(jq):
        tiles = _key_tiles(jq)
        qb = qbt_ref[0]

        def bias_fn(tile):
            return jnp.where(_admissible(tile, jq), 0.0, -jnp.inf) if tile is tiles[-1] else None

        def qk_fn(h, tile):
            kh = kb_ref[0, _rows(tile), h * B_PAD:(h + 1) * B_PAD]
            return jnp.dot(kh, qb[h * B_PAD:(h + 1) * B_PAD], preferred_element_type=F32)

        m = _logits_pass(tiles, qk_fn, bias_fn, s_sc)
        _exp_pv(tiles, m, vbt_ref, gbt_ref, out_ref, s_sc)

    _for_each_query_block(seq_len, run)


def _out_kernel(x_ref, mat_ref, mbt_ref, w_ref, out_ref):
    mixed_t = jnp.concatenate([mat_ref[0], mbt_ref[0]], axis=0)
    y = lax.dot_general(mixed_t, w_ref[...], (((0,), (0,)), ((), ())),
                        preferred_element_type=F32)
    out_ref[0] = x_ref[0] + y


def _rope_tables_t(seq_len, dim):
    half = dim // 2
    freqs = jnp.power(ROPE_THETA, -jnp.arange(half, dtype=F32) * 2.0 / dim)
    ang = jnp.arange(seq_len, dtype=jnp.int32).astype(F32)[None, :] * freqs[:, None]
    return jnp.cos(ang), jnp.sin(ang)


def _params(sem):
    return pltpu.CompilerParams(dimension_semantics=sem, vmem_limit_bytes=V7X_VMEM_LIMIT)


def _layer(h, norm_gain, w_in, a_q_norm, a_k_norm, b_q_latent_norm, b_kv_latent_norm,
           w_uq, w_ukv, b_q_norm, b_k_norm, w_out):
    B, S, D = h.shape
    assert D == D_MODEL and S % TM == 0 and S % TO == 0 and TM % TK == 0 and TK % TQ == 0 and S <= 2048
    nst, nkt, nqb = S // TM, S // TK, S // TQ
    vext = HEADS * V_EXT

    w_t = w_in.T.astype(BF16)
    wuq_t = jnp.pad(w_uq.reshape(Q_LORA, HEADS, B_QK), ((0, 0), (0, 0), (0, B_PAD - B_QK)))
    wuq_t = wuq_t.reshape(Q_LORA, HEADS * B_PAD).T.astype(BF16)
    wukv = w_ukv.reshape(KV_LORA, HEADS, B_NOPE + HEAD_DIM)
    wuk_t = wukv[:, :, :B_NOPE].reshape(KV_LORA, WIDTH).T.astype(BF16)
    wuv_t = wukv[:, :, B_NOPE:].reshape(KV_LORA, WIDTH).T.astype(BF16)
    lanes = lambda g: jnp.broadcast_to(g.astype(F32)[:, None], (g.shape[0], TM))
    pad_b = lambda g: jnp.pad(g, (0, B_PAD - B_QK))
    c64, s64 = _rope_tables_t(S, HEAD_DIM)
    c32, s32 = _rope_tables_t(S, B_ROPE)

    const = lambda shape: pl.BlockSpec(shape, lambda b, i: (0,) * len(shape))
    tok_t = lambda c: pl.BlockSpec((1, c, TM), lambda b, i: (b, 0, i))
    tok = lambda c: pl.BlockSpec((1, TM, c), lambda b, i: (b, i, 0))
    tiles = lambda c: pl.BlockSpec((1, TM // TK, c, TK), lambda b, i: (b, i, 0, 0))
    tab = lambda r: pl.BlockSpec((r, TM), lambda b, i: (0, i))
    sds = jax.ShapeDtypeStruct

    (qat, ka, vat, gat, qit, ki, wit, qbt, kb, vbt, gbt) = pl.pallas_call(
        _proj_kernel,
        grid=(B, nst),
        in_specs=[tok(D), const((1, D)), const((O_END, D)), const((HEADS * B_PAD, Q_LORA)),
                  const((WIDTH, KV_LORA)), const((WIDTH, KV_LORA)),
                  const((64, TM)), const((64, TM)), const((Q_LORA, TM)), const((KV_LORA, TM)),
                  const((B_PAD, TM)), const((B_PAD, TM)),
                  tab(32), tab(32), tab(16), tab(16)],
        out_specs=[tok_t(WIDTH), tok(WIDTH), tiles(vext), tok_t(WIDTH), tok_t(WIDTH), tok(128),
                   tok_t(HEADS), tok_t(HEADS * B_PAD), tok(HEADS * B_PAD), tiles(vext), tok_t(WIDTH)],
        out_shape=[sds((B, WIDTH, S), BF16), sds((B, S, WIDTH), BF16), sds((B, nkt, vext, TK), BF16),
                   sds((B, WIDTH, S), BF16), sds((B, WIDTH, S), BF16), sds((B, S, 128), BF16),
                   sds((B, HEADS, S), F32), sds((B, HEADS * B_PAD, S), BF16),
                   sds((B, S, HEADS * B_PAD), BF16), sds((B, nkt, vext, TK), BF16),
                   sds((B, WIDTH, S), BF16)],
        compiler_params=_params(("arbitrary", "arbitrary")),
        name="proj",
    )(h, norm_gain.reshape(1, D), w_t, wuq_t, wuk_t, wuv_t,
      lanes(a_q_norm), lanes(a_k_norm), lanes(b_q_latent_norm), lanes(b_kv_latent_norm),
      lanes(pad_b(b_q_norm)), lanes(pad_b(b_k_norm)), c64, s64, c32, s32)

    per_b = lambda *shape: pl.BlockSpec((1,) + shape, lambda b, j: (b,) + (0,) * len(shape))
    qblk = lambda c: pl.BlockSpec((1, c, TQ), lambda b, j: (b, 0, j))
    attn_scratch = [pltpu.VMEM((HEADS, S, TQ), F32)]

    mixed_a = pl.pallas_call(
        functools.partial(_dsa_kernel, S),
        grid=(B, nqb),
        in_specs=[per_b(S, 128), per_b(S, WIDTH), per_b(nkt, vext, TK),
                  qblk(WIDTH), qblk(HEADS), qblk(WIDTH), qblk(WIDTH)],
        out_specs=qblk(WIDTH),
        out_shape=sds((B, WIDTH, S), BF16),
        scratch_shapes=[pltpu.VMEM((S, TQ), F32), pltpu.VMEM((S, TQ), I16),
                        pltpu.VMEM((S, TQ), I16)] + attn_scratch,
        compiler_params=_params(("arbitrary", "arbitrary")),
        name="dsa",
    )(ki, ka, vat, qit, wit, qat, gat)

    mixed_b = pl.pallas_call(
        functools.partial(_mla_kernel, S),
        grid=(B, nqb),
        in_specs=[per_b(S, HEADS * B_PAD), per_b(nkt, vext, TK), qblk(HEADS * B_PAD), qblk(WIDTH)],
        out_specs=qblk(WIDTH),
        out_shape=sds((B, WIDTH, S), BF16),
        scratch_shapes=attn_scratch,
        compiler_params=_params(("arbitrary", "arbitrary")),
        name="mla",
    )(kb, vbt, qbt, gbt)

    return pl.pallas_call(
        _out_kernel,
        grid=(B, S // TO),
        in_specs=[pl.BlockSpec((1, TO, D), lambda b, i: (b, i, 0)),
                  pl.BlockSpec((1, WIDTH, TO), lambda b, i: (b, 0, i)),
                  pl.BlockSpec((1, WIDTH, TO), lambda b, i: (b, 0, i)), const((2 * WIDTH, D))],
        out_specs=pl.BlockSpec((1, TO, D), lambda b, i: (b, i, 0)),
        out_shape=sds((B, S, D), F32),
        compiler_params=_params(("arbitrary", "arbitrary")),
        name="outproj",
    )(h, mixed_a, mixed_b, w_out.astype(BF16))


def kernel(x, norm_gain, w_in, a_q_norm, a_k_norm, b_q_latent_norm, b_kv_latent_norm,
           w_uq, w_ukv, b_q_norm, b_k_norm, w_out):
    h = x
    for l in range(norm_gain.shape[0]):
        h = _layer(h, norm_gain[l], w_in[l], a_q_norm[l], a_k_norm[l], b_q_latent_norm[l],
                   b_kv_latent_norm[l], w_uq[l], w_ukv[l], b_q_norm[l], b_k_norm[l], w_out[l])
    return h
```

```python
import functools

import numpy as np
import jax
import jax.numpy as jnp
from jax import lax
from jax.experimental import pallas as pl
from jax.experimental.pallas import tpu as pltpu

F32 = jnp.float32
BF16 = jnp.bfloat16
I32 = jnp.int32
I16 = jnp.int16

D_MODEL = 1024
CHUNK = 64
ROPE_THETA = 10000.0
RMS_EPS = 1e-6
NEG_INF = -1e30
HEADS = 8
HEAD_DIM = 64
WIDTH = HEADS * HEAD_DIM
IDX_DIM = 64
IDX_ROPE_DIM = 32
TOPK_MAX = 256
B_NOPE = 64
B_ROPE = 32
B_QK = B_NOPE + B_ROPE
B_PAD = 128
Q_LORA = 384
KV_LORA = 256
V_EXT = 80

O_QA, O_KA, O_VA, O_GA = 0, 512, 1024, 1536
O_QI, O_KI, O_WI = 2048, 2560, 2624
O_CQ, O_CKV, O_KR, O_GB, O_END = 2632, 3016, 3272, 3304, 3816

TM = 512
TO = 1024
TQ = 256
TK = 512
V7X_VMEM_LIMIT = 56 * 1024 * 1024

INT_MIN = -(2 ** 31)
F32_LOWEST = float(np.finfo(np.float32).min)
LOG2E = float(np.log2(np.e))
KEY_POS_INF = 0x7F800000
KEY_NEG_INF = INT_MIN + 0x00800000
KEY_MIN_NORMAL = 0x00800000
_NEG_BITS = int(np.array(NEG_INF, np.float32).view(np.int32))
NEG_KEY = int(np.int32(np.int64(INT_MIN) - np.int64(_NEG_BITS)))


def _rope_rows(a, b, cos, sin):
    return a * cos - b * sin, b * cos + a * sin


def _head_norm_rope_t(xh, gain, cos, sin, n_true, rope_lo, half):
    ssq = jnp.sum(xh * xh, axis=0, keepdims=True)
    y = (xh * lax.rsqrt(ssq * (1.0 / n_true) + RMS_EPS)) * gain
    ra, rb = _rope_rows(y[rope_lo:rope_lo + half], y[rope_lo + half:rope_lo + 2 * half], cos, sin)
    parts = []
    if rope_lo:
        parts.append(y[:rope_lo])
    parts += [ra, rb]
    if rope_lo + 2 * half < y.shape[0]:
        parts.append(y[rope_lo + 2 * half:])
    return jnp.concatenate(parts, axis=0)


def _silu(g):
    return g * (1.0 / (1.0 + jnp.exp(-g)))


def _key_to_float(key):
    key = jnp.clip(key, KEY_NEG_INF, KEY_POS_INF)
    key = jnp.where(jnp.abs(key) < KEY_MIN_NORMAL, 0, key)
    return lax.bitcast_convert_type(jnp.where(key < 0, INT_MIN - key, key), F32)


def _next_key_up(key):
    nxt = jnp.minimum(key, KEY_POS_INF) + 1
    return jnp.where(jnp.logical_and(nxt > 0, nxt < KEY_MIN_NORMAL), KEY_MIN_NORMAL, nxt)


def _proj_kernel(x_ref, ng_ref, w_ref, wuq_ref, wuk_ref, wuv_ref,
                 gaq_ref, gak_ref, gql_ref, gkvl_ref, gbq_ref, gbk_ref,
                 c64_ref, s64_ref, c32_ref, s32_ref,
                 qat_ref, ka_ref, vat_ref, gat_ref, qit_ref, ki_ref, wit_ref,
                 qbt_ref, kb_ref, vbt_ref, gbt_ref):
    x = x_ref[0]
    ms = jnp.mean(x * x, axis=-1, keepdims=True)
    xb = ((x * lax.rsqrt(ms + RMS_EPS)) * ng_ref[...]).astype(BF16)

    def proj_t(lo, hi):
        return lax.dot_general(w_ref[lo:hi, :], xb, (((1,), (1,)), ((), ())),
                               preferred_element_type=F32)

    c64, s64 = c64_ref[...], s64_ref[...]
    c32, s32 = c32_ref[...], s32_ref[...]
    nkt = TM // TK
    ones_rows = jnp.ones((V_EXT - HEAD_DIM, TM), F32)

    def store_tiles(ref, val):
        ext = jnp.concatenate(
            [r for h in range(HEADS) for r in (val[h * 64:(h + 1) * 64], ones_rows)], axis=0).astype(BF16)
        for t in range(nkt):
            ref[0, t] = ext[:, t * TK:(t + 1) * TK]

    qk = proj_t(O_QA, O_VA)
    qa = qk[:WIDTH]
    gaq = gaq_ref[...]
    qa = jnp.concatenate(
        [_head_norm_rope_t(qa[h * 64:(h + 1) * 64], gaq, c64, s64, 64, 0, 32) for h in range(HEADS)], axis=0)
    qat_ref[0] = (qa * (HEAD_DIM ** -0.5 * LOG2E)).astype(BF16)

    ka = qk[WIDTH:]
    gak = gak_ref[...]
    ka = jnp.concatenate(
        [_head_norm_rope_t(ka[h * 64:(h + 1) * 64], gak, c64, s64, 64, 0, 32) for h in range(HEADS)], axis=0)
    ka_ref[0] = ka.T.astype(BF16)

    vg = proj_t(O_VA, O_QI)
    store_tiles(vat_ref, vg[:WIDTH])
    gat_ref[0] = _silu(vg[WIDTH:]).astype(BF16)

    mid = proj_t(O_QI, O_GB)
    qi = mid[:O_KI - O_QI]
    qi_parts = []
    for h in range(HEADS):
        qh = qi[h * 64:(h + 1) * 64]
        ra, rb = _rope_rows(qh[0:16], qh[16:32], c32, s32)
        qi_parts += [ra, rb, qh[32:]]
    qit_ref[0] = jnp.concatenate(qi_parts, axis=0).astype(BF16)

    kw = mid[O_KI - O_QI:O_CQ - O_QI]
    ra, rb = _rope_rows(kw[0:16], kw[16:32], c32, s32)
    ki = jnp.concatenate([ra, rb, kw[32:64], jnp.zeros((64, TM), F32)], axis=0)
    ki_ref[0] = ki.T.astype(BF16)
    wit_ref[0] = kw[64:72]

    lat = mid[O_CQ - O_QI:]
    cq = lat[:Q_LORA]
    cq = (cq * lax.rsqrt(jnp.mean(cq * cq, axis=0, keepdims=True) + RMS_EPS)) * gql_ref[...]
    qb = jnp.dot(wuq_ref[...], cq.astype(BF16), preferred_element_type=F32)
    gbq = gbq_ref[...]
    qb = jnp.concatenate(
        [_head_norm_rope_t(qb[h * B_PAD:(h + 1) * B_PAD], gbq, c32, s32, B_QK, B_NOPE, 16)
         for h in range(HEADS)], axis=0)
    qbt_ref[0] = (qb * (B_QK ** -0.5 * LOG2E)).astype(BF16)

    ckv = lat[Q_LORA:Q_LORA + KV_LORA]
    ckv = ((ckv * lax.rsqrt(jnp.mean(ckv * ckv, axis=0, keepdims=True) + RMS_EPS)) * gkvl_ref[...]).astype(BF16)
    kr = lat[Q_LORA + KV_LORA:]
    kn = jnp.dot(wuk_ref[...], ckv, preferred_element_type=F32)
    gbk = gbk_ref[...]
    zpad = jnp.zeros((B_PAD - B_QK, TM), F32)
    kb = jnp.concatenate(
        [_head_norm_rope_t(jnp.concatenate([kn[h * 64:(h + 1) * 64], kr, zpad], axis=0),
                           gbk, c32, s32, B_QK, B_NOPE, 16) for h in range(HEADS)], axis=0)
    kb_ref[0] = kb.T.astype(BF16)
    vb = jnp.dot(wuv_ref[...], ckv, preferred_element_type=F32)
    store_tiles(vbt_ref, vb)

    gbt_ref[0] = _silu(proj_t(O_GB, O_END)).astype(BF16)


def _key_tiles(jq):
    n_keys = (jq + 1) * TQ
    tiles = [(t * TK, TK) for t in range(n_keys // TK)]
    if n_keys % TK:
        tiles.append((n_keys - n_keys % TK, n_keys % TK))
    return tiles


def _rows(tile):
    return slice(tile[0], tile[0] + tile[1])


def _logits_pass(tiles, qk_fn, bias_fn, s_sc):
    mrun = [jnp.full((8, TQ), F32_LOWEST, F32) for _ in range(HEADS)]
    for tile in tiles:
        bias = bias_fn(tile)
        for h in range(HEADS):
            s = qk_fn(h, tile)
            if bias is not None:
                s = s + bias
            s_sc[h, _rows(tile), :] = s
            mrun[h] = jnp.maximum(mrun[h], jnp.max(s.reshape(tile[1] // 8, 8, TQ), axis=0))
    return [jnp.max(mrun[h], axis=0, keepdims=True) for h in range(HEADS)]


def _exp_pv(tiles, m, vt_ref, gate_ref, out_ref, s_sc):
    for h in range(HEADS):
        acc = jnp.zeros((V_EXT, TQ), F32)
        for start, size in tiles:
            p = jnp.exp2((s_sc[h, start:start + size, :] - m[h]).astype(BF16))
            acc = acc + jnp.dot(vt_ref[0, start // TK, h * V_EXT:(h + 1) * V_EXT, :size], p,
                                preferred_element_type=F32)
        o = acc[:HEAD_DIM] * (1.0 / acc[HEAD_DIM:HEAD_DIM + 1])
        g = gate_ref[0, h * 64:(h + 1) * 64, :].astype(F32)
        out_ref[0, h * 64:(h + 1) * 64, :] = (o * g).astype(BF16)


def _for_each_query_block(seq_len, run):
    j = pl.program_id(1)
    for jq in range(seq_len // TQ):
        pl.when(j == jq)(functools.partial(run, jq))


def _for_each_tile_count(seq_len, run):
    nk = (pl.program_id(1) * TQ) // TK + 1
    for n in range(1, seq_len // TK + 1):
        pl.when(nk == n)(functools.partial(run, n))


def _admissible(tile, jq):
    key = lax.broadcasted_iota(I32, (tile[1], TQ), 0) + tile[0]
    qpos = lax.broadcasted_iota(I32, (1, TQ), 1) + jq * TQ
    return key < ((qpos >> 6) + 1) * CHUNK


def _dsa_kernel(seq_len, ki_ref, ka_ref, vat_ref, qit_ref, wit_ref, qat_ref, gat_ref, out_ref,
                sc_sc, hi_sc, lo_sc, s_sc):
    topk = min(TOPK_MAX, seq_len // 4)
    idx_scale = (IDX_DIM * HEADS) ** -0.5
    neg_hi, neg_lo = NEG_KEY >> 16, (NEG_KEY & 0xFFFF) - 32768

    def run(nk):
        tiles = [(t * TK, TK) for t in range(nk)]
        jq = pl.program_id(1)
        n_keys = nk * TK
        n_extra = seq_len - n_keys
        idx_bits = (n_keys - 1).bit_length()
        zeros64 = jnp.zeros((64, TQ), BF16)

        def key_index(tile):
            return lax.broadcasted_iota(I32, (tile[1], TQ), 0) + tile[0]

        qi = qit_ref[0]
        w = wit_ref[0]
        qpads = [jnp.concatenate([qi[h * 64:(h + 1) * 64], zeros64], axis=0) for h in range(HEADS)]
        for tile in tiles:
            kt_tile = ki_ref[0, _rows(tile), :]
            acc = jnp.zeros((tile[1], TQ), F32)
            for h in range(HEADS):
                rel = jnp.dot(kt_tile, qpads[h], preferred_element_type=F32)
                acc = acc + jnp.maximum(rel, 0.0) * w[h:h + 1, :]
            score = acc * idx_scale
            if tile is tiles[-1]:
                score = jnp.where(_admissible(tile, jq), score, NEG_INF)
            sc_sc[_rows(tile), :] = score
            bits = lax.bitcast_convert_type(score, I32)
            key = jnp.where(bits < 0, INT_MIN - bits, bits)
            hi_sc[_rows(tile), :] = (key >> 16).astype(I16)
            lo_sc[_rows(tile), :] = ((key & 0xFFFF) - 32768).astype(I16)

        def count(pred):
            part = jnp.zeros((8, TQ), I32)
            for tile in tiles:
                hit = pred(sc_sc[_rows(tile), :], tile).astype(I32)
                part = part + jnp.sum(hit.reshape(tile[1] // 8, 8, TQ), axis=0)
            return jnp.sum(part, axis=0, keepdims=True)

        def count_ge(cand):
            return count(lambda s, tile: s >= cand) + jnp.where(NEG_INF >= cand, n_extra, 0)

        def count16(ref, pred):
            parts = []
            for tile in tiles:
                hit = jnp.where(pred(ref[_rows(tile), :]), jnp.int16(1), jnp.int16(0))
                parts += [hit[r * 16:(r + 1) * 16] for r in range(tile[1] // 16)]
            while len(parts) > 1:
                odd = parts[-1:] if len(parts) % 2 else []
                parts = [a + b for a, b in zip(parts[0::2], parts[1::2])] + odd
            return jnp.sum(parts[0].astype(I32), axis=0, keepdims=True)

        def kth_largest16(ref, need, extra_ge):
            def bit_body(i, t_u):
                c_u = t_u | jnp.left_shift(jnp.int32(1), 15 - i)
                c_s = c_u - 32768
                c16 = c_s.astype(I16)
                cnt = count16(ref, lambda x: x >= c16) + extra_ge(c_s)
                return jnp.where(cnt >= need, c_u, t_u)
            return lax.fori_loop(0, 16, bit_body, jnp.zeros((1, TQ), I32)) - 32768

        t_hi = kth_largest16(hi_sc, topk, lambda c: jnp.where(neg_hi >= c, n_extra, 0))
        t_hi16 = t_hi.astype(I16)
        above = count16(hi_sc, lambda x: x > t_hi16) + jnp.where(neg_hi > t_hi, n_extra, 0)
        for tile in tiles:
            lo_sc[_rows(tile), :] = jnp.where(hi_sc[_rows(tile), :] == t_hi16, lo_sc[_rows(tile), :],
                                              jnp.int16(-32768))
        t_lo = kth_largest16(
            lo_sc, topk - above,
            lambda c: jnp.where(jnp.logical_and(t_hi == neg_hi, neg_lo >= c), n_extra, 0))
        key_fast = t_hi * 65536 + (t_lo + 32768)

        cnt_fast = count_ge(_key_to_float(key_fast))
        cnt_next = count_ge(_key_to_float(_next_key_up(key_fast)))
        certified = jnp.min(jnp.where(jnp.logical_and(cnt_fast >= topk, cnt_next < topk), 1, 0)) > 0

        def float_search():
            def bit_body(i, t_u):
                c_u = t_u | jnp.left_shift(jnp.int32(1), 31 - i)
                return jnp.where(count_ge(_key_to_float(c_u ^ INT_MIN)) >= topk, c_u, t_u)
            key = lax.fori_loop(0, 32, bit_body, jnp.zeros((1, TQ), I32)) ^ INT_MIN
            return key, count_ge(_key_to_float(key))

        thr_key, cnt_ge = lax.cond(certified, lambda: (key_fast, cnt_fast), float_search)
        thr = _key_to_float(thr_key)
        has_tie = jnp.max(jnp.where(cnt_ge != topk, 1, 0)) > 0

        def tie_limit():
            cnt_gt = count(lambda s, tile: s > thr) + jnp.where(NEG_INF > thr, n_extra, 0)
            need = topk - cnt_gt

            def idx_body(i, p):
                c = p | jnp.left_shift(jnp.int32(1), idx_bits - 1 - i)
                below = count(lambda s, tile: jnp.logical_and(s == thr, key_index(tile) < c))
                return jnp.where(below < need, c, p)

            return lax.fori_loop(0, idx_bits, idx_body, jnp.zeros((1, TQ), I32))

        p_lim = lax.cond(has_tie, tie_limit, lambda: jnp.full((1, TQ), seq_len, I32))

        qa = qat_ref[0]
        qa_pads = []
        for h in range(HEADS):
            qh = qa[h * 64:(h + 1) * 64]
            qa_pads.append(jnp.concatenate([zeros64, qh] if h % 2 else [qh, zeros64], axis=0))

        def bias_fn(tile):
            s = sc_sc[_rows(tile), :]
            sel = jnp.logical_or(s > thr, jnp.logical_and(s == thr, key_index(tile) <= p_lim))
            if tile is tiles[-1]:
                sel = jnp.logical_and(sel, _admissible(tile, jq))
            return jnp.where(sel, 0.0, -jnp.inf)

        def qk_fn(h, tile):
            kpair = ka_ref[0, _rows(tile), (h // 2) * 128:(h // 2 + 1) * 128]
            return jnp.dot(kpair, qa_pads[h], preferred_element_type=F32)

        m = _logits_pass(tiles, qk_fn, bias_fn, s_sc)
        _exp_pv(tiles, m, vat_ref, gat_ref, out_ref, s_sc)

    _for_each_tile_count(seq_len, run)


def _mla_kernel(seq_len, kb_ref, vbt_ref, qbt_ref, gbt_ref, out_ref, s_sc):
    def run(jq):
        tiles = _key_tiles(jq)
        qb = qbt_ref[0]

        def bias_fn(tile):
            return jnp.where(_admissible(tile, jq), 0.0, -jnp.inf) if tile is tiles[-1] else None

        def qk_fn(h, tile):
            kh = kb_ref[0, _rows(tile), h * B_PAD:(h + 1) * B_PAD]
            return jnp.dot(kh, qb[h * B_PAD:(h + 1) * B_PAD], preferred_element_type=F32)

        m = _logits_pass(tiles, qk_fn, bias_fn, s_sc)
        _exp_pv(tiles, m, vbt_ref, gbt_ref, out_ref, s_sc)

    _for_each_query_block(seq_len, run)


def _out_kernel(x_ref, mat_ref, mbt_ref, w_ref, out_ref):
    mixed_t = jnp.concatenate([mat_ref[0], mbt_ref[0]], axis=0)
    y = lax.dot_general(mixed_t, w_ref[...], (((0,), (0,)), ((), ())),
                        preferred_element_type=F32)
    out_ref[0] = x_ref[0] + y


def _rope_tables_t(seq_len, dim):
    half = dim // 2
    freqs = jnp.power(ROPE_THETA, -jnp.arange(half, dtype=F32) * 2.0 / dim)
    ang = jnp.arange(seq_len, dtype=jnp.int32).astype(F32)[None, :] * freqs[:, None]
    return jnp.cos(ang), jnp.sin(ang)


def _params(sem):
    return pltpu.CompilerParams(dimension_semantics=sem, vmem_limit_bytes=V7X_VMEM_LIMIT)


def _layer(h, norm_gain, w_in, a_q_norm, a_k_norm, b_q_latent_norm, b_kv_latent_norm,
           w_uq, w_ukv, b_q_norm, b_k_norm, w_out):
    B, S, D = h.shape
    assert D == D_MODEL and S % TM == 0 and S % TO == 0 and TM % TK == 0 and TK % TQ == 0 and S <= 2048
    nst, nkt, nqb = S // TM, S // TK, S // TQ
    vext = HEADS * V_EXT

    w_t = w_in.T.astype(BF16)
    wuq_t = jnp.pad(w_uq.reshape(Q_LORA, HEADS, B_QK), ((0, 0), (0, 0), (0, B_PAD - B_QK)))
    wuq_t = wuq_t.reshape(Q_LORA, HEADS * B_PAD).T.astype(BF16)
    wukv = w_ukv.reshape(KV_LORA, HEADS, B_NOPE + HEAD_DIM)
    wuk_t = wukv[:, :, :B_NOPE].reshape(KV_LORA, WIDTH).T.astype(BF16)
    wuv_t = wukv[:, :, B_NOPE:].reshape(KV_LORA, WIDTH).T.astype(BF16)
    lanes = lambda g: jnp.broadcast_to(g.astype(F32)[:, None], (g.shape[0], TM))
    pad_b = lambda g: jnp.pad(g, (0, B_PAD - B_QK))
    c64, s64 = _rope_tables_t(S, HEAD_DIM)
    c32, s32 = _rope_tables_t(S, B_ROPE)

    const = lambda shape: pl.BlockSpec(shape, lambda b, i: (0,) * len(shape))
    tok_t = lambda c: pl.BlockSpec((1, c, TM), lambda b, i: (b, 0, i))
    tok = lambda c: pl.BlockSpec((1, TM, c), lambda b, i: (b, i, 0))
    tiles = lambda c: pl.BlockSpec((1, TM // TK, c, TK), lambda b, i: (b, i, 0, 0))
    tab = lambda r: pl.BlockSpec((r, TM), lambda b, i: (0, i))
    sds = jax.ShapeDtypeStruct

    (qat, ka, vat, gat, qit, ki, wit, qbt, kb, vbt, gbt) = pl.pallas_call(
        _proj_kernel,
        grid=(B, nst),
        in_specs=[tok(D), const((1, D)), const((O_END, D)), const((HEADS * B_PAD, Q_LORA)),
                  const((WIDTH, KV_LORA)), const((WIDTH, KV_LORA)),
                  const((64, TM)), const((64, TM)), const((Q_LORA, TM)), const((KV_LORA, TM)),
                  const((B_PAD, TM)), const((B_PAD, TM)),
                  tab(32), tab(32), tab(16), tab(16)],
        out_specs=[tok_t(WIDTH), tok(WIDTH), tiles(vext), tok_t(WIDTH), tok_t(WIDTH), tok(128),
                   tok_t(HEADS), tok_t(HEADS * B_PAD), tok(HEADS * B_PAD), tiles(vext), tok_t(WIDTH)],
        out_shape=[sds((B, WIDTH, S), BF16), sds((B, S, WIDTH), BF16), sds((B, nkt, vext, TK), BF16),
                   sds((B, WIDTH, S), BF16), sds((B, WIDTH, S), BF16), sds((B, S, 128), BF16),
                   sds((B, HEADS, S), F32), sds((B, HEADS * B_PAD, S), BF16),
                   sds((B, S, HEADS * B_PAD), BF16), sds((B, nkt, vext, TK), BF16),
                   sds((B, WIDTH, S), BF16)],
        compiler_params=_params(("arbitrary", "arbitrary")),
        name="proj",
    )(h, norm_gain.reshape(1, D), w_t, wuq_t, wuk_t, wuv_t,
      lanes(a_q_norm), lanes(a_k_norm), lanes(b_q_latent_norm), lanes(b_kv_latent_norm),
      lanes(pad_b(b_q_norm)), lanes(pad_b(b_k_norm)), c64, s64, c32, s32)

    per_b = lambda *shape: pl.BlockSpec((1,) + shape, lambda b, j: (b,) + (0,) * len(shape))
    qblk = lambda c: pl.BlockSpec((1, c, TQ), lambda b, j: (b, 0, j))
    attn_scratch = [pltpu.VMEM((HEADS, S, TQ), F32)]

    mixed_a = pl.pallas_call(
        functools.partial(_dsa_kernel, S),
        grid=(B, nqb),
        in_specs=[per_b(S, 128), per_b(S, WIDTH), per_b(nkt, vext, TK),
                  qblk(WIDTH), qblk(HEADS), qblk(WIDTH), qblk(WIDTH)],
        out_specs=qblk(WIDTH),
        out_shape=sds((B, WIDTH, S), BF16),
        scratch_shapes=[pltpu.VMEM((S, TQ), F32), pltpu.VMEM((S, TQ), I16),
                        pltpu.VMEM((S, TQ), I16)] + attn_scratch,
        compiler_params=_params(("arbitrary", "arbitrary")),
        name="dsa",
    )(ki, ka, vat, qit, wit, qat, gat)

    mixed_b = pl.pallas_call(
        functools.partial(_mla_kernel, S),
        grid=(B, nqb),
        in_specs=[per_b(S, HEADS * B_PAD), per_b(nkt, vext, TK), qblk(HEADS * B_PAD), qblk(WIDTH)],
        out_specs=qblk(WIDTH),
        out_shape=sds((B, WIDTH, S), BF16),
        scratch_shapes=attn_scratch,
        compiler_params=_params(("arbitrary", "arbitrary")),
        name="mla",
    )(kb, vbt, qbt, gbt)

    return pl.pallas_call(
        _out_kernel,
        grid=(B, S // TO),
        in_specs=[pl.BlockSpec((1, TO, D), lambda b, i: (b, i, 0)),
                  pl.BlockSpec((1, WIDTH, TO), lambda b, i: (b, 0, i)),
                  pl.BlockSpec((1, WIDTH, TO), lambda b, i: (b, 0, i)), const((2 * WIDTH, D))],
        out_specs=pl.BlockSpec((1, TO, D), lambda b, i: (b, i, 0)),
        out_shape=sds((B, S, D), F32),
        compiler_params=_params(("arbitrary", "arbitrary")),
        name="outproj",
    )(h, mixed_a, mixed_b, w_out.astype(BF16))


def kernel(x, norm_gain, w_in, a_q_norm, a_k_norm, b_q_latent_norm, b_kv_latent_norm,
           w_uq, w_ukv, b_q_norm, b_k_norm, w_out):
    h = x
    for l in range(norm_gain.shape[0]):
        h = _layer(h, norm_gain[l], w_in[l], a_q_norm[l], a_k_norm[l], b_q_latent_norm[l],
                   b_kv_latent_norm[l], w_uq[l], w_ukv[l], b_q_norm[l], b_k_norm[l], w_out[l])
    return h
```

```python
import functools

import numpy as np
import jax
import jax.numpy as jnp
from jax import lax
from jax.experimental import pallas as pl
from jax.experimental.pallas import tpu as pltpu

F32 = jnp.float32
BF16 = jnp.bfloat16
I32 = jnp.int32
I16 = jnp.int16

D_MODEL = 1024
CHUNK = 64
ROPE_THETA = 10000.0
RMS_EPS = 1e-6
NEG_INF = -1e30
HEADS = 8
HEAD_DIM = 64
WIDTH = HEADS * HEAD_DIM
IDX_DIM = 64
IDX_ROPE_DIM = 32
TOPK_MAX = 256
B_NOPE = 64
B_ROPE = 32
B_QK = B_NOPE + B_ROPE
B_PAD = 128
Q_LORA = 384
KV_LORA = 256
V_EXT = 80

O_QA, O_KA, O_VA, O_GA = 0, 512, 1024, 1536
O_QI, O_KI, O_WI = 2048, 2560, 2624
O_CQ, O_CKV, O_KR, O_GB, O_END = 2632, 3016, 3272, 3304, 3816

TM = 512
TO = 1024
TQ = 256
TK = 512
V7X_VMEM_LIMIT = 56 * 1024 * 1024

INT_MIN = -(2 ** 31)
F32_LOWEST = float(np.finfo(np.float32).min)
LOG2E = float(np.log2(np.e))
KEY_POS_INF = 0x7F800000
KEY_NEG_INF = INT_MIN + 0x00800000
KEY_MIN_NORMAL = 0x00800000
_NEG_BITS = int(np.array(NEG_INF, np.float32).view(np.int32))
NEG_KEY = int(np.int32(np.int64(INT_MIN) - np.int64(_NEG_BITS)))


def _rope_rows(a, b, cos, sin):
    return a * cos - b * sin, b * cos + a * sin


def _head_norm_rope_t(xh, gain, cos, sin, n_true, rope_lo, half):
    ssq = jnp.sum(xh * xh, axis=0, keepdims=True)
    y = (xh * lax.rsqrt(ssq * (1.0 / n_true) + RMS_EPS)) * gain
    ra, rb = _rope_rows(y[rope_lo:rope_lo + half], y[rope_lo + half:rope_lo + 2 * half], cos, sin)
    parts = []
    if rope_lo:
        parts.append(y[:rope_lo])
    parts += [ra, rb]
    if rope_lo + 2 * half < y.shape[0]:
        parts.append(y[rope_lo + 2 * half:])
    return jnp.concatenate(parts, axis=0)


def _silu(g):
    return g * (1.0 / (1.0 + jnp.exp(-g)))


def _key_to_float(key):
    key = jnp.clip(key, KEY_NEG_INF, KEY_POS_INF)
    key = jnp.where(jnp.abs(key) < KEY_MIN_NORMAL, 0, key)
    return lax.bitcast_convert_type(jnp.where(key < 0, INT_MIN - key, key), F32)


def _next_key_up(key):
    nxt = jnp.minimum(key, KEY_POS_INF) + 1
    return jnp.where(jnp.logical_and(nxt > 0, nxt < KEY_MIN_NORMAL), KEY_MIN_NORMAL, nxt)


def _proj_kernel(x_ref, ng_ref, w_ref, wuq_ref, wuk_ref, wuv_ref,
                 gaq_ref, gak_ref, gql_ref, gkvl_ref, gbq_ref, gbk_ref,
                 c64_ref, s64_ref, c32_ref, s32_ref,
                 qat_ref, ka_ref, vat_ref, gat_ref, qit_ref, ki_ref, wit_ref,
                 qbt_ref, kb_ref, vbt_ref, gbt_ref):
    x = x_ref[0]
    ms = jnp.mean(x * x, axis=-1, keepdims=True)
    xb = ((x * lax.rsqrt(ms + RMS_EPS)) * ng_ref[...]).astype(BF16)

    def proj_t(lo, hi):
        return lax.dot_general(w_ref[lo:hi, :], xb, (((1,), (1,)), ((), ())),
                               preferred_element_type=F32)

    c64, s64 = c64_ref[...], s64_ref[...]
    c32, s32 = c32_ref[...], s32_ref[...]
    nkt = TM // TK
    ones_rows = jnp.ones((V_EXT - HEAD_DIM, TM), F32)

    def store_tiles(ref, val):
        ext = jnp.concatenate(
            [r for h in range(HEADS) for r in (val[h * 64:(h + 1) * 64], ones_rows)], axis=0).astype(BF16)
        for t in range(nkt):
            ref[0, t] = ext[:, t * TK:(t + 1) * TK]

    qk = proj_t(O_QA, O_VA)
    qa = qk[:WIDTH]
    gaq = gaq_ref[...]
    qa = jnp.concatenate(
        [_head_norm_rope_t(qa[h * 64:(h + 1) * 64], gaq, c64, s64, 64, 0, 32) for h in range(HEADS)], axis=0)
    qat_ref[0] = (qa * (HEAD_DIM ** -0.5 * LOG2E)).astype(BF16)

    ka = qk[WIDTH:]
    gak = gak_ref[...]
    ka = jnp.concatenate(
        [_head_norm_rope_t(ka[h * 64:(h + 1) * 64], gak, c64, s64, 64, 0, 32) for h in range(HEADS)], axis=0)
    ka_ref[0] = ka.T.astype(BF16)

    vg = proj_t(O_VA, O_QI)
    store_tiles(vat_ref, vg[:WIDTH])
    gat_ref[0] = _silu(vg[WIDTH:]).astype(BF16)

    mid = proj_t(O_QI, O_GB)
    qi = mid[:O_KI - O_QI]
    qi_parts = []
    for h in range(HEADS):
        qh = qi[h * 64:(h + 1) * 64]
        ra, rb = _rope_rows(qh[0:16], qh[16:32], c32, s32)
        qi_parts += [ra, rb, qh[32:]]
    qit_ref[0] = jnp.concatenate(qi_parts, axis=0).astype(BF16)

    kw = mid[O_KI - O_QI:O_CQ - O_QI]
    ra, rb = _rope_rows(kw[0:16], kw[16:32], c32, s32)
    ki = jnp.concatenate([ra, rb, kw[32:64], jnp.zeros((64, TM), F32)], axis=0)
    ki_ref[0] = ki.T.astype(BF16)
    wit_ref[0] = kw[64:72]

    lat = mid[O_CQ - O_QI:]
    cq = lat[:Q_LORA]
    cq = (cq * lax.rsqrt(jnp.mean(cq * cq, axis=0, keepdims=True) + RMS_EPS)) * gql_ref[...]
    qb = jnp.dot(wuq_ref[...], cq.astype(BF16), preferred_element_type=F32)
    gbq = gbq_ref[...]
    qb = jnp.concatenate(
        [_head_norm_rope_t(qb[h * B_PAD:(h + 1) * B_PAD], gbq, c32, s32, B_QK, B_NOPE, 16)
         for h in range(HEADS)], axis=0)
    qbt_ref[0] = (qb * (B_QK ** -0.5 * LOG2E)).astype(BF16)

    ckv = lat[Q_LORA:Q_LORA + KV_LORA]
    ckv = ((ckv * lax.rsqrt(jnp.mean(ckv * ckv, axis=0, keepdims=True) + RMS_EPS)) * gkvl_ref[...]).astype(BF16)
    kr = lat[Q_LORA + KV_LORA:]
    kn = jnp.dot(wuk_ref[...], ckv, preferred_element_type=F32)
    gbk = gbk_ref[...]
    zpad = jnp.zeros((B_PAD - B_QK, TM), F32)
    kb = jnp.concatenate(
        [_head_norm_rope_t(jnp.concatenate([kn[h * 64:(h + 1) * 64], kr, zpad], axis=0),
                           gbk, c32, s32, B_QK, B_NOPE, 16) for h in range(HEADS)], axis=0)
    kb_ref[0] = kb.T.astype(BF16)
    vb = jnp.dot(wuv_ref[...], ckv, preferred_element_type=F32)
    store_tiles(vbt_ref, vb)

    gbt_ref[0] = _silu(proj_t(O_GB, O_END)).astype(BF16)


def _key_tiles(jq):
    n_keys = (jq + 1) * TQ
    tiles = [(t * TK, TK) for t in range(n_keys // TK)]
    if n_keys % TK:
        tiles.append((n_keys - n_keys % TK, n_keys % TK))
    return tiles


def _rows(tile):
    return slice(tile[0], tile[0] + tile[1])


def _init_max():
    return [jnp.full((8, TQ), F32_LOWEST, F32) for _ in range(HEADS)]


def _logits_tiles(tiles, qk_fn, bias_fn, s_sc, mrun):
    mrun = list(mrun)
    for tile in tiles:
        bias = bias_fn(tile)
        for h in range(HEADS):
            s = qk_fn(h, tile)
            if bias is not None:
                s = s + bias
            s_sc[h, _rows(tile), :] = s
            mrun[h] = jnp.maximum(mrun[h], jnp.max(s.reshape(tile[1] // 8, 8, TQ), axis=0))
    return mrun


def _row_max(mrun):
    return [jnp.max(mr, axis=0, keepdims=True) for mr in mrun]


def _pv_tiles(tiles, m, vt_ref, s_sc, accs):
    accs = list(accs)
    for h in range(HEADS):
        for start, size in tiles:
            p = jnp.exp2((s_sc[h, start:start + size, :] - m[h]).astype(BF16))
            lanes = slice(start % TK, start % TK + size)
            accs[h] = accs[h] + jnp.dot(vt_ref[0, start // TK, h * V_EXT:(h + 1) * V_EXT, lanes], p,
                                        preferred_element_type=F32)
    return accs


def _init_acc():
    return [jnp.zeros((V_EXT, TQ), F32) for _ in range(HEADS)]


def _finish(accs, gate_ref, out_ref):
    for h in range(HEADS):
        o = accs[h][:HEAD_DIM] * (1.0 / accs[h][HEAD_DIM:HEAD_DIM + 1])
        g = gate_ref[0, h * 64:(h + 1) * 64, :].astype(F32)
        out_ref[0, h * 64:(h + 1) * 64, :] = (o * g).astype(BF16)


def _for_each_query_block(seq_len, run):
    j = pl.program_id(1)
    for jq in range(seq_len // TQ):
        pl.when(j == jq)(functools.partial(run, jq))


def _for_each_tile_count(seq_len, run):
    nk = (pl.program_id(1) * TQ) // TK + 1
    for n in range(1, seq_len // TK + 1):
        pl.when(nk == n)(functools.partial(run, n))


def _admissible(tile, jq):
    key = lax.broadcasted_iota(I32, (tile[1], TQ), 0) + tile[0]
    qpos = lax.broadcasted_iota(I32, (1, TQ), 1) + jq * TQ
    return key < ((qpos >> 6) + 1) * CHUNK


def _dsa_kernel(seq_len, ki_ref, ka_ref, vat_ref, qit_ref, wit_ref, qat_ref, gat_ref, out_ref,
                sc_sc, hi_sc, lo_sc, s_sc):
    topk = min(TOPK_MAX, seq_len // 4)
    idx_scale = (IDX_DIM * HEADS) ** -0.5
    neg_hi, neg_lo = NEG_KEY >> 16, (NEG_KEY & 0xFFFF) - 32768

    def run(nk):
        jq = pl.program_id(1)
        tile_a, tile_b = ((nk - 1) * TK, TQ), ((nk - 1) * TK + TQ, TQ)
        head_tiles = [(t * TK, TK) for t in range(nk - 1)] + [tile_a]
        tiles = head_tiles + [tile_b]
        has_b = (jq * TQ) % TK != 0
        n_keys = nk * TK
        n_extra = seq_len - n_keys
        idx_bits = (n_keys - 1).bit_length()
        zeros64 = jnp.zeros((64, TQ), BF16)

        def key_index(tile):
            return lax.broadcasted_iota(I32, (tile[1], TQ), 0) + tile[0]

        qi = qit_ref[0]
        w = wit_ref[0]
        qpads = [jnp.concatenate([qi[h * 64:(h + 1) * 64], zeros64], axis=0) for h in range(HEADS)]
        def score_tile(tile):
            kt_tile = ki_ref[0, _rows(tile), :]
            acc = jnp.zeros((tile[1], TQ), F32)
            for h in range(HEADS):
                rel = jnp.dot(kt_tile, qpads[h], preferred_element_type=F32)
                acc = acc + jnp.maximum(rel, 0.0) * w[h:h + 1, :]
            score = acc * idx_scale
            if tile in (tile_a, tile_b):
                score = jnp.where(_admissible(tile, jq), score, NEG_INF)
            sc_sc[_rows(tile), :] = score
            bits = lax.bitcast_convert_type(score, I32)
            key = jnp.where(bits < 0, INT_MIN - bits, bits)
            hi_sc[_rows(tile), :] = (key >> 16).astype(I16)
            lo_sc[_rows(tile), :] = ((key & 0xFFFF) - 32768).astype(I16)

        for tile in head_tiles:
            score_tile(tile)
        pl.when(has_b)(functools.partial(score_tile, tile_b))

        @pl.when(jnp.logical_not(has_b))
        def _():
            sc_sc[_rows(tile_b), :] = jnp.full((TQ, TQ), NEG_INF, F32)
            hi_sc[_rows(tile_b), :] = jnp.full((TQ, TQ), neg_hi, I16)
            lo_sc[_rows(tile_b), :] = jnp.full((TQ, TQ), neg_lo, I16)

        def count(pred):
            part = jnp.zeros((8, TQ), I32)
            for tile in tiles:
                hit = pred(sc_sc[_rows(tile), :], tile).astype(I32)
                part = part + jnp.sum(hit.reshape(tile[1] // 8, 8, TQ), axis=0)
            return jnp.sum(part, axis=0, keepdims=True)

        def count_ge(cand):
            return count(lambda s, tile: s >= cand) + jnp.where(NEG_INF >= cand, n_extra, 0)

        def count16(ref, pred):
            parts = []
            for tile in tiles:
                hit = jnp.where(pred(ref[_rows(tile), :]), jnp.int16(1), jnp.int16(0))
                parts += [hit[r * 16:(r + 1) * 16] for r in range(tile[1] // 16)]
            while len(parts) > 1:
                odd = parts[-1:] if len(parts) % 2 else []
                parts = [a + b for a, b in zip(parts[0::2], parts[1::2])] + odd
            return jnp.sum(parts[0].astype(I32), axis=0, keepdims=True)

        def kth_largest16(ref, need, extra_ge):
            def bit_body(i, t_u):
                c_u = t_u | jnp.left_shift(jnp.int32(1), 15 - i)
                c_s = c_u - 32768
                c16 = c_s.astype(I16)
                cnt = count16(ref, lambda x: x >= c16) + extra_ge(c_s)
                return jnp.where(cnt >= need, c_u, t_u)
            return lax.fori_loop(0, 16, bit_body, jnp.zeros((1, TQ), I32)) - 32768

        t_hi = kth_largest16(hi_sc, topk, lambda c: jnp.where(neg_hi >= c, n_extra, 0))
        t_hi16 = t_hi.astype(I16)
        above = count16(hi_sc, lambda x: x > t_hi16) + jnp.where(neg_hi > t_hi, n_extra, 0)
        for tile in tiles:
            lo_sc[_rows(tile), :] = jnp.where(hi_sc[_rows(tile), :] == t_hi16, lo_sc[_rows(tile), :],
                                              jnp.int16(-32768))
        t_lo = kth_largest16(
            lo_sc, topk - above,
            lambda c: jnp.where(jnp.logical_and(t_hi == neg_hi, neg_lo >= c), n_extra, 0))
        key_fast = t_hi * 65536 + (t_lo + 32768)

        cnt_fast = count_ge(_key_to_float(key_fast))
        cnt_next = count_ge(_key_to_float(_next_key_up(key_fast)))
        certified = jnp.min(jnp.where(jnp.logical_and(cnt_fast >= topk, cnt_next < topk), 1, 0)) > 0

        def float_search():
            def bit_body(i, t_u):
                c_u = t_u | jnp.left_shift(jnp.int32(1), 31 - i)
                return jnp.where(count_ge(_key_to_float(c_u ^ INT_MIN)) >= topk, c_u, t_u)
            key = lax.fori_loop(0, 32, bit_body, jnp.zeros((1, TQ), I32)) ^ INT_MIN
            return key, count_ge(_key_to_float(key))

        thr_key, cnt_ge = lax.cond(certified, lambda: (key_fast, cnt_fast), float_search)
        thr = _key_to_float(thr_key)
        has_tie = jnp.max(jnp.where(cnt_ge != topk, 1, 0)) > 0

        def tie_limit():
            cnt_gt = count(lambda s, tile: s > thr) + jnp.where(NEG_INF > thr, n_extra, 0)
            need = topk - cnt_gt

            def idx_body(i, p):
                c = p | jnp.left_shift(jnp.int32(1), idx_bits - 1 - i)
                below = count(lambda s, tile: jnp.logical_and(s == thr, key_index(tile) < c))
                return jnp.where(below < need, c, p)

            return lax.fori_loop(0, idx_bits, idx_body, jnp.zeros((1, TQ), I32))

        p_lim = lax.cond(has_tie, tie_limit, lambda: jnp.full((1, TQ), seq_len, I32))

        qa = qat_ref[0]
        qa_pads = []
        for h in range(HEADS):
            qh = qa[h * 64:(h + 1) * 64]
            qa_pads.append(jnp.concatenate([zeros64, qh] if h % 2 else [qh, zeros64], axis=0))

        def bias_fn(tile):
            s = sc_sc[_rows(tile), :]
            sel = jnp.logical_or(s > thr, jnp.logical_and(s == thr, key_index(tile) <= p_lim))
            if tile in (tile_a, tile_b):
                sel = jnp.logical_and(sel, _admissible(tile, jq))
            return jnp.where(sel, 0.0, -jnp.inf)

        def qk_fn(h, tile):
            kpair = ka_ref[0, _rows(tile), (h // 2) * 128:(h // 2 + 1) * 128]
            return jnp.dot(kpair, qa_pads[h], preferred_element_type=F32)

        mrun = _logits_tiles(head_tiles, qk_fn, bias_fn, s_sc, _init_max())
        mrun = lax.cond(has_b, lambda mr: _logits_tiles([tile_b], qk_fn, bias_fn, s_sc, mr),
                        lambda mr: list(mr), mrun)
        m = _row_max(mrun)
        accs = _pv_tiles(head_tiles, m, vat_ref, s_sc, _init_acc())
        accs = lax.cond(has_b, lambda ac: _pv_tiles([tile_b], m, vat_ref, s_sc, ac),
                        lambda ac: list(ac), accs)
        _finish(accs, gat_ref, out_ref)

    _for_each_tile_count(seq_len, run)


def _mla_kernel(seq_len, kb_ref, vbt_ref, qbt_ref, gbt_ref, out_ref, s_sc):
    def run(jq):
        tiles = _key_tiles(jq)
        qb = qbt_ref[0]

        def bias_fn(tile):
            return jnp.where(_admissible(tile, jq), 0.0, -jnp.inf) if tile is tiles[-1] else None

        def qk_fn(h, tile):
            kh = kb_ref[0, _rows(tile), h * B_PAD:(h + 1) * B_PAD]
            return jnp.dot(kh, qb[h * B_PAD:(h + 1) * B_PAD], preferred_element_type=F32)

        m = _row_max(_logits_tiles(tiles, qk_fn, bias_fn, s_sc, _init_max()))
        _finish(_pv_tiles(tiles, m, vbt_ref, s_sc, _init_acc()), gbt_ref, out_ref)

    _for_each_query_block(seq_len, run)


def _out_kernel(x_ref, mat_ref, mbt_ref, w_ref, out_ref):
    mixed_t = jnp.concatenate([mat_ref[0], mbt_ref[0]], axis=0)
    y = lax.dot_general(mixed_t, w_ref[...], (((0,), (0,)), ((), ())),
                        preferred_element_type=F32)
    out_ref[0] = x_ref[0] + y


def _rope_tables_t(seq_len, dim):
    half = dim // 2
    freqs = jnp.power(ROPE_THETA, -jnp.arange(half, dtype=F32) * 2.0 / dim)
    ang = jnp.arange(seq_len, dtype=jnp.int32).astype(F32)[None, :] * freqs[:, None]
    return jnp.cos(ang), jnp.sin(ang)


def _params(sem):
    return pltpu.CompilerParams(dimension_semantics=sem, vmem_limit_bytes=V7X_VMEM_LIMIT)


def _layer(h, norm_gain, w_in, a_q_norm, a_k_norm, b_q_latent_norm, b_kv_latent_norm,
           w_uq, w_ukv, b_q_norm, b_k_norm, w_out):
    B, S, D = h.shape
    assert D == D_MODEL and S % TM == 0 and S % TO == 0 and TM % TK == 0 and TK % TQ == 0 and S <= 2048
    nst, nkt, nqb = S // TM, S // TK, S // TQ
    vext = HEADS * V_EXT

    w_t = w_in.T.astype(BF16)
    wuq_t = jnp.pad(w_uq.reshape(Q_LORA, HEADS, B_QK), ((0, 0), (0, 0), (0, B_PAD - B_QK)))
    wuq_t = wuq_t.reshape(Q_LORA, HEADS * B_PAD).T.astype(BF16)
    wukv = w_ukv.reshape(KV_LORA, HEADS, B_NOPE + HEAD_DIM)
    wuk_t = wukv[:, :, :B_NOPE].reshape(KV_LORA, WIDTH).T.astype(BF16)
    wuv_t = wukv[:, :, B_NOPE:].reshape(KV_LORA, WIDTH).T.astype(BF16)
    lanes = lambda g: jnp.broadcast_to(g.astype(F32)[:, None], (g.shape[0], TM))
    pad_b = lambda g: jnp.pad(g, (0, B_PAD - B_QK))
    c64, s64 = _rope_tables_t(S, HEAD_DIM)
    c32, s32 = _rope_tables_t(S, B_ROPE)

    const = lambda shape: pl.BlockSpec(shape, lambda b, i: (0,) * len(shape))
    tok_t = lambda c: pl.BlockSpec((1, c, TM), lambda b, i: (b, 0, i))
    tok = lambda c: pl.BlockSpec((1, TM, c), lambda b, i: (b, i, 0))
    tiles = lambda c: pl.BlockSpec((1, TM // TK, c, TK), lambda b, i: (b, i, 0, 0))
    tab = lambda r: pl.BlockSpec((r, TM), lambda b, i: (0, i))
    sds = jax.ShapeDtypeStruct

    (qat, ka, vat, gat, qit, ki, wit, qbt, kb, vbt, gbt) = pl.pallas_call(
        _proj_kernel,
        grid=(B, nst),
        in_specs=[tok(D), const((1, D)), const((O_END, D)), const((HEADS * B_PAD, Q_LORA)),
                  const((WIDTH, KV_LORA)), const((WIDTH, KV_LORA)),
                  const((64, TM)), const((64, TM)), const((Q_LORA, TM)), const((KV_LORA, TM)),
                  const((B_PAD, TM)), const((B_PAD, TM)),
                  tab(32), tab(32), tab(16), tab(16)],
        out_specs=[tok_t(WIDTH), tok(WIDTH), tiles(vext), tok_t(WIDTH), tok_t(WIDTH), tok(128),
                   tok_t(HEADS), tok_t(HEADS * B_PAD), tok(HEADS * B_PAD), tiles(vext), tok_t(WIDTH)],
        out_shape=[sds((B, WIDTH, S), BF16), sds((B, S, WIDTH), BF16), sds((B, nkt, vext, TK), BF16),
                   sds((B, WIDTH, S), BF16), sds((B, WIDTH, S), BF16), sds((B, S, 128), BF16),
                   sds((B, HEADS, S), F32), sds((B, HEADS * B_PAD, S), BF16),
                   sds((B, S, HEADS * B_PAD), BF16), sds((B, nkt, vext, TK), BF16),
                   sds((B, WIDTH, S), BF16)],
        compiler_params=_params(("arbitrary", "arbitrary")),
        name="proj",
    )(h, norm_gain.reshape(1, D), w_t, wuq_t, wuk_t, wuv_t,
      lanes(a_q_norm), lanes(a_k_norm), lanes(b_q_latent_norm), lanes(b_kv_latent_norm),
      lanes(pad_b(b_q_norm)), lanes(pad_b(b_k_norm)), c64, s64, c32, s32)

    per_b = lambda *shape: pl.BlockSpec((1,) + shape, lambda b, j: (b,) + (0,) * len(shape))
    qblk = lambda c: pl.BlockSpec((1, c, TQ), lambda b, j: (b, 0, j))
    attn_scratch = [pltpu.VMEM((HEADS, S, TQ), F32)]

    mixed_a = pl.pallas_call(
        functools.partial(_dsa_kernel, S),
        grid=(B, nqb),
        in_specs=[per_b(S, 128), per_b(S, WIDTH), per_b(nkt, vext, TK),
                  qblk(WIDTH), qblk(HEADS), qblk(WIDTH), qblk(WIDTH)],
        out_specs=qblk(WIDTH),
        out_shape=sds((B, WIDTH, S), BF16),
        scratch_shapes=[pltpu.VMEM((S, TQ), F32), pltpu.VMEM((S, TQ), I16),
                        pltpu.VMEM((S, TQ), I16)] + attn_scratch,
        compiler_params=_params(("arbitrary", "arbitrary")),
        name="dsa",
    )(ki, ka, vat, qit, wit, qat, gat)

    mixed_b = pl.pallas_call(
        functools.partial(_mla_kernel, S),
        grid=(B, nqb),
        in_specs=[per_b(S, HEADS * B_PAD), per_b(nkt, vext, TK), qblk(HEADS * B_PAD), qblk(WIDTH)],
        out_specs=qblk(WIDTH),
        out_shape=sds((B, WIDTH, S), BF16),
        scratch_shapes=attn_scratch,
        compiler_params=_params(("arbitrary", "arbitrary")),
        name="mla",
    )(kb, vbt, qbt, gbt)

    return pl.pallas_call(
        _out_kernel,
        grid=(B, S // TO),
        in_specs=[pl.BlockSpec((1, TO, D), lambda b, i: (b, i, 0)),
                  pl.BlockSpec((1, WIDTH, TO), lambda b, i: (b, 0, i)),
                  pl.BlockSpec((1, WIDTH, TO), lambda b, i: (b, 0, i)), const((2 * WIDTH, D))],
        out_specs=pl.BlockSpec((1, TO, D), lambda b, i: (b, i, 0)),
        out_shape=sds((B, S, D), F32),
        compiler_params=_params(("arbitrary", "arbitrary")),
        name="outproj",
    )(h, mixed_a, mixed_b, w_out.astype(BF16))


def kernel(x, norm_gain, w_in, a_q_norm, a_k_norm, b_q_latent_norm, b_kv_latent_norm,
           w_uq, w_ukv, b_q_norm, b_k_norm, w_out):
    h = x
    for l in range(norm_gain.shape[0]):
        h = _layer(h, norm_gain[l], w_in[l], a_q_norm[l], a_k_norm[l], b_q_latent_norm[l],
                   b_kv_latent_norm[l], w_uq[l], w_ukv[l], b_q_norm[l], b_k_norm[l], w_out[l])
    return h
```

```python
import functools

import numpy as np
import jax
import jax.numpy as jnp
from jax import lax
from jax.experimental import pallas as pl
from jax.experimental.pallas import tpu as pltpu

F32 = jnp.float32
BF16 = jnp.bfloat16
I32 = jnp.int32
I16 = jnp.int16

D_MODEL = 1024
CHUNK = 64
ROPE_THETA = 10000.0
RMS_EPS = 1e-6
NEG_INF = -1e30
HEADS = 8
HEAD_DIM = 64
WIDTH = HEADS * HEAD_DIM
IDX_DIM = 64
IDX_ROPE_DIM = 32
TOPK_MAX = 256
B_NOPE = 64
B_ROPE = 32
B_QK = B_NOPE + B_ROPE
B_PAD = 128
Q_LORA = 384
KV_LORA = 256
V_EXT = 80

O_QA, O_KA, O_VA, O_GA = 0, 512, 1024, 1536
O_QI, O_KI, O_WI = 2048, 2560, 2624
O_CQ, O_CKV, O_KR, O_GB, O_END = 2632, 3016, 3272, 3304, 3816

TM = 512
TO = 1024
TQ = 256
TK = 512
V7X_VMEM_LIMIT = 56 * 1024 * 1024
DSA_EXACT_TILES = 2

INT_MIN = -(2 ** 31)
F32_LOWEST = float(np.finfo(np.float32).min)
LOG2E = float(np.log2(np.e))
KEY_POS_INF = 0x7F800000
KEY_NEG_INF = INT_MIN + 0x00800000
KEY_MIN_NORMAL = 0x00800000
_NEG_BITS = int(np.array(NEG_INF, np.float32).view(np.int32))
NEG_KEY = int(np.int32(np.int64(INT_MIN) - np.int64(_NEG_BITS)))


def _rope_rows(a, b, cos, sin):
    return a * cos - b * sin, b * cos + a * sin


def _head_norm_rope_t(xh, gain, cos, sin, n_true, rope_lo, half):
    ssq = jnp.sum(xh * xh, axis=0, keepdims=True)
    y = (xh * lax.rsqrt(ssq * (1.0 / n_true) + RMS_EPS)) * gain
    ra, rb = _rope_rows(y[rope_lo:rope_lo + half], y[rope_lo + half:rope_lo + 2 * half], cos, sin)
    parts = []
    if rope_lo:
        parts.append(y[:rope_lo])
    parts += [ra, rb]
    if rope_lo + 2 * half < y.shape[0]:
        parts.append(y[rope_lo + 2 * half:])
    return jnp.concatenate(parts, axis=0)


def _silu(g):
    return g * (1.0 / (1.0 + jnp.exp(-g)))


def _key_to_float(key):
    key = jnp.clip(key, KEY_NEG_INF, KEY_POS_INF)
    key = jnp.where(jnp.abs(key) < KEY_MIN_NORMAL, 0, key)
    return lax.bitcast_convert_type(jnp.where(key < 0, INT_MIN - key, key), F32)


def _next_key_up(key):
    nxt = jnp.minimum(key, KEY_POS_INF) + 1
    return jnp.where(jnp.logical_and(nxt > 0, nxt < KEY_MIN_NORMAL), KEY_MIN_NORMAL, nxt)


def _proj_kernel(x_ref, ng_ref, w_ref, wuq_ref, wuk_ref, wuv_ref,
                 gaq_ref, gak_ref, gql_ref, gkvl_ref, gbq_ref, gbk_ref,
                 c64_ref, s64_ref, c32_ref, s32_ref,
                 qat_ref, ka_ref, vat_ref, gat_ref, qit_ref, ki_ref, wit_ref,
                 qbt_ref, kb_ref, vbt_ref, gbt_ref):
    x = x_ref[0]
    ms = jnp.mean(x * x, axis=-1, keepdims=True)
    xb = ((x * lax.rsqrt(ms + RMS_EPS)) * ng_ref[...]).astype(BF16)

    def proj_t(lo, hi):
        return lax.dot_general(w_ref[lo:hi, :], xb, (((1,), (1,)), ((), ())),
                               preferred_element_type=F32)

    c64, s64 = c64_ref[...], s64_ref[...]
    c32, s32 = c32_ref[...], s32_ref[...]
    nkt = TM // TK
    ones_rows = jnp.ones((V_EXT - HEAD_DIM, TM), F32)

    def store_tiles(ref, val):
        ext = jnp.concatenate(
            [r for h in range(HEADS) for r in (val[h * 64:(h + 1) * 64], ones_rows)], axis=0).astype(BF16)
        for t in range(nkt):
            ref[0, t] = ext[:, t * TK:(t + 1) * TK]

    qk = proj_t(O_QA, O_VA)
    qa = qk[:WIDTH]
    gaq = gaq_ref[...]
    qa = jnp.concatenate(
        [_head_norm_rope_t(qa[h * 64:(h + 1) * 64], gaq, c64, s64, 64, 0, 32) for h in range(HEADS)], axis=0)
    qat_ref[0] = (qa * (HEAD_DIM ** -0.5 * LOG2E)).astype(BF16)

    ka = qk[WIDTH:]
    gak = gak_ref[...]
    ka = jnp.concatenate(
        [_head_norm_rope_t(ka[h * 64:(h + 1) * 64], gak, c64, s64, 64, 0, 32) for h in range(HEADS)], axis=0)
    ka_ref[0] = ka.T.astype(BF16)

    vg = proj_t(O_VA, O_QI)
    store_tiles(vat_ref, vg[:WIDTH])
    gat_ref[0] = _silu(vg[WIDTH:]).astype(BF16)

    mid = proj_t(O_QI, O_GB)
    qi = mid[:O_KI - O_QI]
    qi_parts = []
    for h in range(HEADS):
        qh = qi[h * 64:(h + 1) * 64]
        ra, rb = _rope_rows(qh[0:16], qh[16:32], c32, s32)
        qi_parts += [ra, rb, qh[32:]]
    qit_ref[0] = jnp.concatenate(qi_parts, axis=0).astype(BF16)

    kw = mid[O_KI - O_QI:O_CQ - O_QI]
    ra, rb = _rope_rows(kw[0:16], kw[16:32], c32, s32)
    ki = jnp.concatenate([ra, rb, kw[32:64], jnp.zeros((64, TM), F32)], axis=0)
    ki_ref[0] = ki.T.astype(BF16)
    wit_ref[0] = kw[64:72]

    lat = mid[O_CQ - O_QI:]
    cq = lat[:Q_LORA]
    cq = (cq * lax.rsqrt(jnp.mean(cq * cq, axis=0, keepdims=True) + RMS_EPS)) * gql_ref[...]
    qb = jnp.dot(wuq_ref[...], cq.astype(BF16), preferred_element_type=F32)
    gbq = gbq_ref[...]
    qb = jnp.concatenate(
        [_head_norm_rope_t(qb[h * B_PAD:(h + 1) * B_PAD], gbq, c32, s32, B_QK, B_NOPE, 16)
         for h in range(HEADS)], axis=0)
    qbt_ref[0] = (qb * (B_QK ** -0.5 * LOG2E)).astype(BF16)

    ckv = lat[Q_LORA:Q_LORA + KV_LORA]
    ckv = ((ckv * lax.rsqrt(jnp.mean(ckv * ckv, axis=0, keepdims=True) + RMS_EPS)) * gkvl_ref[...]).astype(BF16)
    kr = lat[Q_LORA + KV_LORA:]
    kn = jnp.dot(wuk_ref[...], ckv, preferred_element_type=F32)
    gbk = gbk_ref[...]
    zpad = jnp.zeros((B_PAD - B_QK, TM), F32)
    kb = jnp.concatenate(
        [_head_norm_rope_t(jnp.concatenate([kn[h * 64:(h + 1) * 64], kr, zpad], axis=0),
                           gbk, c32, s32, B_QK, B_NOPE, 16) for h in range(HEADS)], axis=0)
    kb_ref[0] = kb.T.astype(BF16)
    vb = jnp.dot(wuv_ref[...], ckv, preferred_element_type=F32)
    store_tiles(vbt_ref, vb)

    gbt_ref[0] = _silu(proj_t(O_GB, O_END)).astype(BF16)


def _key_tiles(jq):
    n_keys = (jq + 1) * TQ
    tiles = [(t * TK, TK) for t in range(n_keys // TK)]
    if n_keys % TK:
        tiles.append((n_keys - n_keys % TK, n_keys % TK))
    return tiles


def _rows(tile):
    return slice(tile[0], tile[0] + tile[1])


def _logits_pass(tiles, qk_fn, bias_fn, s_sc):
    mrun = [jnp.full((8, TQ), F32_LOWEST, F32) for _ in range(HEADS)]
    for tile in tiles:
        bias = bias_fn(tile)
        for h in range(HEADS):
            s = qk_fn(h, tile)
            if bias is not None:
                s = s + bias
            s_sc[h, _rows(tile), :] = s
            mrun[h] = jnp.maximum(mrun[h], jnp.max(s.reshape(tile[1] // 8, 8, TQ), axis=0))
    return [jnp.max(mrun[h], axis=0, keepdims=True) for h in range(HEADS)]


def _exp_pv(tiles, m, vt_ref, gate_ref, out_ref, s_sc):
    for h in range(HEADS):
        acc = jnp.zeros((V_EXT, TQ), F32)
        for start, size in tiles:
            p = jnp.exp2((s_sc[h, start:start + size, :] - m[h]).astype(BF16))
            acc = acc + jnp.dot(vt_ref[0, start // TK, h * V_EXT:(h + 1) * V_EXT, :size], p,
                                preferred_element_type=F32)
        o = acc[:HEAD_DIM] * (1.0 / acc[HEAD_DIM:HEAD_DIM + 1])
        g = gate_ref[0, h * 64:(h + 1) * 64, :].astype(F32)
        out_ref[0, h * 64:(h + 1) * 64, :] = (o * g).astype(BF16)


def _for_each_query_block(seq_len, run):
    j = pl.program_id(1)
    for jq in range(seq_len // TQ):
        pl.when(j == jq)(functools.partial(run, jq))


def _for_each_key_range(seq_len, run):
    j = pl.program_id(1)
    nk = (j * TQ) // TK + 1
    shared = set()
    for jq in range(seq_len // TQ):
        n = (jq * TQ) // TK + 1
        if n <= DSA_EXACT_TILES:
            pl.when(j == jq)(functools.partial(run, _key_tiles(jq), jq))
        elif n not in shared:
            shared.add(n)
            pl.when(nk == n)(functools.partial(run, [(t * TK, TK) for t in range(n)], j))


def _admissible(tile, jq):
    key = lax.broadcasted_iota(I32, (tile[1], TQ), 0) + tile[0]
    qpos = lax.broadcasted_iota(I32, (1, TQ), 1) + jq * TQ
    return key < ((qpos >> 6) + 1) * CHUNK


def _dsa_kernel(seq_len, ki_ref, ka_ref, vat_ref, qit_ref, wit_ref, qat_ref, gat_ref, out_ref,
                sc_sc, hi_sc, lo_sc, s_sc):
    topk = min(TOPK_MAX, seq_len // 4)
    idx_scale = (IDX_DIM * HEADS) ** -0.5
    neg_hi, neg_lo = NEG_KEY >> 16, (NEG_KEY & 0xFFFF) - 32768

    def run(tiles, jq):
        n_keys = tiles[-1][0] + tiles[-1][1]
        n_extra = seq_len - n_keys
        idx_bits = (n_keys - 1).bit_length()
        zeros64 = jnp.zeros((64, TQ), BF16)

        def key_index(tile):
            return lax.broadcasted_iota(I32, (tile[1], TQ), 0) + tile[0]

        qi = qit_ref[0]
        w = wit_ref[0]
        qpads = [jnp.concatenate([qi[h * 64:(h + 1) * 64], zeros64], axis=0) for h in range(HEADS)]
        for tile in tiles:
            kt_tile = ki_ref[0, _rows(tile), :]
            acc = jnp.zeros((tile[1], TQ), F32)
            for h in range(HEADS):
                rel = jnp.dot(kt_tile, qpads[h], preferred_element_type=F32)
                acc = acc + jnp.maximum(rel, 0.0) * w[h:h + 1, :]
            score = acc * idx_scale
            if tile is tiles[-1]:
                score = jnp.where(_admissible(tile, jq), score, NEG_INF)
            sc_sc[_rows(tile), :] = score
            bits = lax.bitcast_convert_type(score, I32)
            key = jnp.where(bits < 0, INT_MIN - bits, bits)
            hi_sc[_rows(tile), :] = (key >> 16).astype(I16)
            lo_sc[_rows(tile), :] = ((key & 0xFFFF) - 32768).astype(I16)

        def count(pred):
            part = jnp.zeros((8, TQ), I32)
            for tile in tiles:
                hit = pred(sc_sc[_rows(tile), :], tile).astype(I32)
                part = part + jnp.sum(hit.reshape(tile[1] // 8, 8, TQ), axis=0)
            return jnp.sum(part, axis=0, keepdims=True)

        def count_ge(cand):
            return count(lambda s, tile: s >= cand) + jnp.where(NEG_INF >= cand, n_extra, 0)

        def count16(ref, pred):
            parts = []
            for tile in tiles:
                hit = jnp.where(pred(ref[_rows(tile), :]), jnp.int16(1), jnp.int16(0))
                parts += [hit[r * 16:(r + 1) * 16] for r in range(tile[1] // 16)]
            while len(parts) > 1:
                odd = parts[-1:] if len(parts) % 2 else []
                parts = [a + b for a, b in zip(parts[0::2], parts[1::2])] + odd
            return jnp.sum(parts[0].astype(I32), axis=0, keepdims=True)

        def kth_largest16(ref, need, extra_ge):
            def bit_body(i, t_u):
                c_u = t_u | jnp.left_shift(jnp.int32(1), 15 - i)
                c_s = c_u - 32768
                c16 = c_s.astype(I16)
                cnt = count16(ref, lambda x: x >= c16) + extra_ge(c_s)
                return jnp.where(cnt >= need, c_u, t_u)
            return lax.fori_loop(0, 16, bit_body, jnp.zeros((1, TQ), I32)) - 32768

        t_hi = kth_largest16(hi_sc, topk, lambda c: jnp.where(neg_hi >= c, n_extra, 0))
        t_hi16 = t_hi.astype(I16)
        above = count16(hi_sc, lambda x: x > t_hi16) + jnp.where(neg_hi > t_hi, n_extra, 0)
        for tile in tiles:
            lo_sc[_rows(tile), :] = jnp.where(hi_sc[_rows(tile), :] == t_hi16, lo_sc[_rows(tile), :],
                                              jnp.int16(-32768))
        t_lo = kth_largest16(
            lo_sc, topk - above,
            lambda c: jnp.where(jnp.logical_and(t_hi == neg_hi, neg_lo >= c), n_extra, 0))
        key_fast = t_hi * 65536 + (t_lo + 32768)

        cnt_fast = count_ge(_key_to_float(key_fast))
        cnt_next = count_ge(_key_to_float(_next_key_up(key_fast)))
        certified = jnp.min(jnp.where(jnp.logical_and(cnt_fast >= topk, cnt_next < topk), 1, 0)) > 0

        def float_search():
            def bit_body(i, t_u):
                c_u = t_u | jnp.left_shift(jnp.int32(1), 31 - i)
                return jnp.where(count_ge(_key_to_float(c_u ^ INT_MIN)) >= topk, c_u, t_u)
            key = lax.fori_loop(0, 32, bit_body, jnp.zeros((1, TQ), I32)) ^ INT_MIN
            return key, count_ge(_key_to_float(key))

        thr_key, cnt_ge = lax.cond(certified, lambda: (key_fast, cnt_fast), float_search)
        thr = _key_to_float(thr_key)
        has_tie = jnp.max(jnp.where(cnt_ge != topk, 1, 0)) > 0

        def tie_limit():
            cnt_gt = count(lambda s, tile: s > thr) + jnp.where(NEG_INF > thr, n_extra, 0)
            need = topk - cnt_gt

            def idx_body(i, p):
                c = p | jnp.left_shift(jnp.int32(1), idx_bits - 1 - i)
                below = count(lambda s, tile: jnp.logical_and(s == thr, key_index(tile) < c))
                return jnp.where(below < need, c, p)

            return lax.fori_loop(0, idx_bits, idx_body, jnp.zeros((1, TQ), I32))

        p_lim = lax.cond(has_tie, tie_limit, lambda: jnp.full((1, TQ), seq_len, I32))

        qa = qat_ref[0]
        qa_pads = []
        for h in range(HEADS):
            qh = qa[h * 64:(h + 1) * 64]
            qa_pads.append(jnp.concatenate([zeros64, qh] if h % 2 else [qh, zeros64], axis=0))

        def bias_fn(tile):
            s = sc_sc[_rows(tile), :]
            sel = jnp.logical_or(s > thr, jnp.logical_and(s == thr, key_index(tile) <= p_lim))
            if tile is tiles[-1]:
                sel = jnp.logical_and(sel, _admissible(tile, jq))
            return jnp.where(sel, 0.0, -jnp.inf)

        def qk_fn(h, tile):
            kpair = ka_ref[0, _rows(tile), (h // 2) * 128:(h // 2 + 1) * 128]
            return jnp.dot(kpair, qa_pads[h], preferred_element_type=F32)

        m = _logits_pass(tiles, qk_fn, bias_fn, s_sc)
        _exp_pv(tiles, m, vat_ref, gat_ref, out_ref, s_sc)

    _for_each_key_range(seq_len, run)


def _mla_kernel(seq_len, kb_ref, vbt_ref, qbt_ref, gbt_ref, out_ref, s_sc):
    def run(jq):
        tiles = _key_tiles(jq)
        qb = qbt_ref[0]

        def bias_fn(tile):
            return jnp.where(_admissible(tile, jq), 0.0, -jnp.inf) if tile is tiles[-1] else None

        def qk_fn(h, tile):
            kh = kb_ref[0, _rows(tile), h * B_PAD:(h + 1) * B_PAD]
            return jnp.dot(kh, qb[h * B_PAD:(h + 1) * B_PAD], preferred_element_type=F32)

        m = _logits_pass(tiles, qk_fn, bias_fn, s_sc)
        _exp_pv(tiles, m, vbt_ref, gbt_ref, out_ref, s_sc)

    _for_each_query_block(seq_len, run)


def _out_kernel(x_ref, mat_ref, mbt_ref, w_ref, out_ref):
    mixed_t = jnp.concatenate([mat_ref[0], mbt_ref[0]], axis=0)
    y = lax.dot_general(mixed_t, w_ref[...], (((0,), (0,)), ((), ())),
                        preferred_element_type=F32)
    out_ref[0] = x_ref[0] + y


def _rope_tables_t(seq_len, dim):
    half = dim // 2
    freqs = jnp.power(ROPE_THETA, -jnp.arange(half, dtype=F32) * 2.0 / dim)
    ang = jnp.arange(seq_len, dtype=jnp.int32).astype(F32)[None, :] * freqs[:, None]
    return jnp.cos(ang), jnp.sin(ang)


def _params(sem):
    return pltpu.CompilerParams(dimension_semantics=sem, vmem_limit_bytes=V7X_VMEM_LIMIT)


def _layer(h, norm_gain, w_in, a_q_norm, a_k_norm, b_q_latent_norm, b_kv_latent_norm,
           w_uq, w_ukv, b_q_norm, b_k_norm, w_out):
    B, S, D = h.shape
    assert D == D_MODEL and S % TM == 0 and S % TO == 0 and TM % TK == 0 and TK % TQ == 0 and S <= 2048
    nst, nkt, nqb = S // TM, S // TK, S // TQ
    vext = HEADS * V_EXT

    w_t = w_in.T.astype(BF16)
    wuq_t = jnp.pad(w_uq.reshape(Q_LORA, HEADS, B_QK), ((0, 0), (0, 0), (0, B_PAD - B_QK)))
    wuq_t = wuq_t.reshape(Q_LORA, HEADS * B_PAD).T.astype(BF16)
    wukv = w_ukv.reshape(KV_LORA, HEADS, B_NOPE + HEAD_DIM)
    wuk_t = wukv[:, :, :B_NOPE].reshape(KV_LORA, WIDTH).T.astype(BF16)
    wuv_t = wukv[:, :, B_NOPE:].reshape(KV_LORA, WIDTH).T.astype(BF16)
    lanes = lambda g: jnp.broadcast_to(g.astype(F32)[:, None], (g.shape[0], TM))
    pad_b = lambda g: jnp.pad(g, (0, B_PAD - B_QK))
    c64, s64 = _rope_tables_t(S, HEAD_DIM)
    c32, s32 = _rope_tables_t(S, B_ROPE)

    const = lambda shape: pl.BlockSpec(shape, lambda b, i: (0,) * len(shape))
    tok_t = lambda c: pl.BlockSpec((1, c, TM), lambda b, i: (b, 0, i))
    tok = lambda c: pl.BlockSpec((1, TM, c), lambda b, i: (b, i, 0))
    tiles = lambda c: pl.BlockSpec((1, TM // TK, c, TK), lambda b, i: (b, i, 0, 0))
    tab = lambda r: pl.BlockSpec((r, TM), lambda b, i: (0, i))
    sds = jax.ShapeDtypeStruct

    (qat, ka, vat, gat, qit, ki, wit, qbt, kb, vbt, gbt) = pl.pallas_call(
        _proj_kernel,
        grid=(B, nst),
        in_specs=[tok(D), const((1, D)), const((O_END, D)), const((HEADS * B_PAD, Q_LORA)),
                  const((WIDTH, KV_LORA)), const((WIDTH, KV_LORA)),
                  const((64, TM)), const((64, TM)), const((Q_LORA, TM)), const((KV_LORA, TM)),
                  const((B_PAD, TM)), const((B_PAD, TM)),
                  tab(32), tab(32), tab(16), tab(16)],
        out_specs=[tok_t(WIDTH), tok(WIDTH), tiles(vext), tok_t(WIDTH), tok_t(WIDTH), tok(128),
                   tok_t(HEADS), tok_t(HEADS * B_PAD), tok(HEADS * B_PAD), tiles(vext), tok_t(WIDTH)],
        out_shape=[sds((B, WIDTH, S), BF16), sds((B, S, WIDTH), BF16), sds((B, nkt, vext, TK), BF16),
                   sds((B, WIDTH, S), BF16), sds((B, WIDTH, S), BF16), sds((B, S, 128), BF16),
                   sds((B, HEADS, S), F32), sds((B, HEADS * B_PAD, S), BF16),
                   sds((B, S, HEADS * B_PAD), BF16), sds((B, nkt, vext, TK), BF16),
                   sds((B, WIDTH, S), BF16)],
        compiler_params=_params(("arbitrary", "arbitrary")),
        name="proj",
    )(h, norm_gain.reshape(1, D), w_t, wuq_t, wuk_t, wuv_t,
      lanes(a_q_norm), lanes(a_k_norm), lanes(b_q_latent_norm), lanes(b_kv_latent_norm),
      lanes(pad_b(b_q_norm)), lanes(pad_b(b_k_norm)), c64, s64, c32, s32)

    per_b = lambda *shape: pl.BlockSpec((1,) + shape, lambda b, j: (b,) + (0,) * len(shape))
    qblk = lambda c: pl.BlockSpec((1, c, TQ), lambda b, j: (b, 0, j))
    attn_scratch = [pltpu.VMEM((HEADS, S, TQ), F32)]

    mixed_a = pl.pallas_call(
        functools.partial(_dsa_kernel, S),
        grid=(B, nqb),
        in_specs=[per_b(S, 128), per_b(S, WIDTH), per_b(nkt, vext, TK),
                  qblk(WIDTH), qblk(HEADS), qblk(WIDTH), qblk(WIDTH)],
        out_specs=qblk(WIDTH),
        out_shape=sds((B, WIDTH, S), BF16),
        scratch_shapes=[pltpu.VMEM((S, TQ), F32), pltpu.VMEM((S, TQ), I16),
                        pltpu.VMEM((S, TQ), I16)] + attn_scratch,
        compiler_params=_params(("arbitrary", "arbitrary")),
        name="dsa",
    )(ki, ka, vat, qit, wit, qat, gat)

    mixed_b = pl.pallas_call(
        functools.partial(_mla_kernel, S),
        grid=(B, nqb),
        in_specs=[per_b(S, HEADS * B_PAD), per_b(nkt, vext, TK), qblk(HEADS * B_PAD), qblk(WIDTH)],
        out_specs=qblk(WIDTH),
        out_shape=sds((B, WIDTH, S), BF16),
        scratch_shapes=attn_scratch,
        compiler_params=_params(("arbitrary", "arbitrary")),
        name="mla",
    )(kb, vbt, qbt, gbt)

    return pl.pallas_call(
        _out_kernel,
        grid=(B, S // TO),
        in_specs=[pl.BlockSpec((1, TO, D), lambda b, i: (b, i, 0)),
                  pl.BlockSpec((1, WIDTH, TO), lambda b, i: (b, 0, i)),
                  pl.BlockSpec((1, WIDTH, TO), lambda b, i: (b, 0, i)), const((2 * WIDTH, D))],
        out_specs=pl.BlockSpec((1, TO, D), lambda b, i: (b, i, 0)),
        out_shape=sds((B, S, D), F32),
        compiler_params=_params(("arbitrary", "arbitrary")),
        name="outproj",
    )(h, mixed_a, mixed_b, w_out.astype(BF16))


def kernel(x, norm_gain, w_in, a_q_norm, a_k_norm, b_q_latent_norm, b_kv_latent_norm,
           w_uq, w_ukv, b_q_norm, b_k_norm, w_out):
    h = x
    for l in range(norm_gain.shape[0]):
        h = _layer(h, norm_gain[l], w_in[l], a_q_norm[l], a_k_norm[l], b_q_latent_norm[l],
                   b_kv_latent_norm[l], w_uq[l], w_ukv[l], b_q_norm[l], b_k_norm[l], w_out[l])
    return h
```

```python
import functools

import numpy as np
import jax
import jax.numpy as jnp
from jax import lax
from jax.experimental import pallas as pl
from jax.experimental.pallas import tpu as pltpu

F32 = jnp.float32
BF16 = jnp.bfloat16
I32 = jnp.int32
I16 = jnp.int16

D_MODEL = 1024
CHUNK = 64
ROPE_THETA = 10000.0
RMS_EPS = 1e-6
NEG_INF = -1e30
HEADS = 8
HEAD_DIM = 64
WIDTH = HEADS * HEAD_DIM
IDX_DIM = 64
IDX_ROPE_DIM = 32
TOPK_MAX = 256
B_NOPE = 64
B_ROPE = 32
B_QK = B_NOPE + B_ROPE
B_PAD = 128
Q_LORA = 384
KV_LORA = 256
V_EXT = 80

O_QA, O_KA, O_VA, O_GA = 0, 512, 1024, 1536
O_QI, O_KI, O_WI = 2048, 2560, 2624
O_CQ, O_CKV, O_KR, O_GB, O_END = 2632, 3016, 3272, 3304, 3816

TM = 512
TO = 2048
TQ = 256
TK = 512
V7X_VMEM_LIMIT = 56 * 1024 * 1024
DSA_EXACT_TILES = 3

INT_MIN = -(2 ** 31)
F32_LOWEST = float(np.finfo(np.float32).min)
LOG2E = float(np.log2(np.e))
KEY_POS_INF = 0x7F800000
KEY_NEG_INF = INT_MIN + 0x00800000
KEY_MIN_NORMAL = 0x00800000
_NEG_BITS = int(np.array(NEG_INF, np.float32).view(np.int32))
NEG_KEY = int(np.int32(np.int64(INT_MIN) - np.int64(_NEG_BITS)))


def _rope_rows(a, b, cos, sin):
    return a * cos - b * sin, b * cos + a * sin


def _head_norm_rope_t(xh, gain, cos, sin, n_true, rope_lo, half):
    ssq = jnp.sum(xh * xh, axis=0, keepdims=True)
    y = (xh * lax.rsqrt(ssq * (1.0 / n_true) + RMS_EPS)) * gain
    ra, rb = _rope_rows(y[rope_lo:rope_lo + half], y[rope_lo + half:rope_lo + 2 * half], cos, sin)
    parts = []
    if rope_lo:
        parts.append(y[:rope_lo])
    parts += [ra, rb]
    if rope_lo + 2 * half < y.shape[0]:
        parts.append(y[rope_lo + 2 * half:])
    return jnp.concatenate(parts, axis=0)


def _silu(g):
    return g * (1.0 / (1.0 + jnp.exp(-g)))


def _key_to_float(key):
    key = jnp.clip(key, KEY_NEG_INF, KEY_POS_INF)
    key = jnp.where(jnp.abs(key) < KEY_MIN_NORMAL, 0, key)
    return lax.bitcast_convert_type(jnp.where(key < 0, INT_MIN - key, key), F32)


def _next_key_up(key):
    nxt = jnp.minimum(key, KEY_POS_INF) + 1
    return jnp.where(jnp.logical_and(nxt > 0, nxt < KEY_MIN_NORMAL), KEY_MIN_NORMAL, nxt)


def _proj_kernel(x_ref, ng_ref, w_ref, wuq_ref, wuk_ref, wuv_ref,
                 gaq_ref, gak_ref, gql_ref, gkvl_ref, gbq_ref, gbk_ref,
                 c64_ref, s64_ref, c32_ref, s32_ref,
                 qat_ref, ka_ref, vat_ref, gat_ref, qit_ref, ki_ref, wit_ref,
                 qbt_ref, kb_ref, vbt_ref, gbt_ref):
    x = x_ref[0]
    ms = jnp.mean(x * x, axis=-1, keepdims=True)
    xb = ((x * lax.rsqrt(ms + RMS_EPS)) * ng_ref[...]).astype(BF16)

    def proj_t(lo, hi):
        return lax.dot_general(w_ref[lo:hi, :], xb, (((1,), (1,)), ((), ())),
                               preferred_element_type=F32)

    c64, s64 = c64_ref[...], s64_ref[...]
    c32, s32 = c32_ref[...], s32_ref[...]
    nkt = TM // TK
    ones_rows = jnp.ones((V_EXT - HEAD_DIM, TM), F32)

    def store_tiles(ref, val):
        ext = jnp.concatenate(
            [r for h in range(HEADS) for r in (val[h * 64:(h + 1) * 64], ones_rows)], axis=0).astype(BF16)
        for t in range(nkt):
            ref[0, t] = ext[:, t * TK:(t + 1) * TK]

    qk = proj_t(O_QA, O_VA)
    vg = proj_t(O_VA, O_QI)
    mid = proj_t(O_QI, O_GB)
    gb = proj_t(O_GB, O_END)
    qa = qk[:WIDTH]
    gaq = gaq_ref[...]
    qa = jnp.concatenate(
        [_head_norm_rope_t(qa[h * 64:(h + 1) * 64], gaq, c64, s64, 64, 0, 32) for h in range(HEADS)], axis=0)
    qat_ref[0] = (qa * (HEAD_DIM ** -0.5 * LOG2E)).astype(BF16)

    ka = qk[WIDTH:]
    gak = gak_ref[...]
    ka = jnp.concatenate(
        [_head_norm_rope_t(ka[h * 64:(h + 1) * 64], gak, c64, s64, 64, 0, 32) for h in range(HEADS)], axis=0)
    ka_ref[0] = ka.T.astype(BF16)

    store_tiles(vat_ref, vg[:WIDTH])
    gat_ref[0] = _silu(vg[WIDTH:]).astype(BF16)

    qi = mid[:O_KI - O_QI]
    qi_parts = []
    for h in range(HEADS):
        qh = qi[h * 64:(h + 1) * 64]
        ra, rb = _rope_rows(qh[0:16], qh[16:32], c32, s32)
        qi_parts += [ra, rb, qh[32:]]
    qit_ref[0] = jnp.concatenate(qi_parts, axis=0).astype(BF16)

    kw = mid[O_KI - O_QI:O_CQ - O_QI]
    ra, rb = _rope_rows(kw[0:16], kw[16:32], c32, s32)
    ki = jnp.concatenate([ra, rb, kw[32:64], jnp.zeros((64, TM), F32)], axis=0)
    ki_ref[0] = ki.T.astype(BF16)
    wit_ref[0] = kw[64:72]

    lat = mid[O_CQ - O_QI:]
    cq = lat[:Q_LORA]
    cq = (cq * lax.rsqrt(jnp.mean(cq * cq, axis=0, keepdims=True) + RMS_EPS)) * gql_ref[...]
    qb = jnp.dot(wuq_ref[...], cq.astype(BF16), preferred_element_type=F32)
    gbq = gbq_ref[...]
    qb = jnp.concatenate(
        [_head_norm_rope_t(qb[h * B_PAD:(h + 1) * B_PAD], gbq, c32, s32, B_QK, B_NOPE, 16)
         for h in range(HEADS)], axis=0)
    qbt_ref[0] = (qb * (B_QK ** -0.5 * LOG2E)).astype(BF16)

    ckv = lat[Q_LORA:Q_LORA + KV_LORA]
    ckv = ((ckv * lax.rsqrt(jnp.mean(ckv * ckv, axis=0, keepdims=True) + RMS_EPS)) * gkvl_ref[...]).astype(BF16)
    kr = lat[Q_LORA + KV_LORA:]
    kn = jnp.dot(wuk_ref[...], ckv, preferred_element_type=F32)
    gbk = gbk_ref[...]
    zpad = jnp.zeros((B_PAD - B_QK, TM), F32)
    kb = jnp.concatenate(
        [_head_norm_rope_t(jnp.concatenate([kn[h * 64:(h + 1) * 64], kr, zpad], axis=0),
                           gbk, c32, s32, B_QK, B_NOPE, 16) for h in range(HEADS)], axis=0)
    kb_ref[0] = kb.T.astype(BF16)
    vb = jnp.dot(wuv_ref[...], ckv, preferred_element_type=F32)
    store_tiles(vbt_ref, vb)

    gbt_ref[0] = _silu(gb).astype(BF16)


def _key_tiles(jq):
    n_keys = (jq + 1) * TQ
    tiles = [(t * TK, TK) for t in range(n_keys // TK)]
    if n_keys % TK:
        tiles.append((n_keys - n_keys % TK, n_keys % TK))
    return tiles


def _rows(tile):
    return slice(tile[0], tile[0] + tile[1])


def _logits_pass(tiles, qk_fn, bias_fn, s_sc):
    mrun = [jnp.full((8, TQ), F32_LOWEST, F32) for _ in range(HEADS)]
    for tile in tiles:
        bias = bias_fn(tile)
        for h in range(HEADS):
            s = qk_fn(h, tile)
            if bias is not None:
                s = s + bias
            s_sc[h, _rows(tile), :] = s
            mrun[h] = jnp.maximum(mrun[h], jnp.max(s.reshape(tile[1] // 8, 8, TQ), axis=0))
    return [jnp.max(mrun[h], axis=0, keepdims=True) for h in range(HEADS)]


def _exp_pv(tiles, m, vt_ref, gate_ref, out_ref, s_sc):
    for h in range(HEADS):
        acc = jnp.zeros((V_EXT, TQ), F32)
        for start, size in tiles:
            p = jnp.exp2((s_sc[h, start:start + size, :] - m[h]).astype(BF16))
            acc = acc + jnp.dot(vt_ref[0, start // TK, h * V_EXT:(h + 1) * V_EXT, :size], p,
                                preferred_element_type=F32)
        o = acc[:HEAD_DIM] * (1.0 / acc[HEAD_DIM:HEAD_DIM + 1])
        g = gate_ref[0, h * 64:(h + 1) * 64, :].astype(F32)
        out_ref[0, h * 64:(h + 1) * 64, :] = (o * g).astype(BF16)


def _for_each_query_block(seq_len, run):
    j = pl.program_id(1)
    for jq in range(seq_len // TQ):
        pl.when(j == jq)(functools.partial(run, jq))


def _for_each_key_range(seq_len, run):
    j = pl.program_id(1)
    nk = (j * TQ) // TK + 1
    shared = set()
    for jq in range(seq_len // TQ):
        n = (jq * TQ) // TK + 1
        if n <= DSA_EXACT_TILES:
            pl.when(j == jq)(functools.partial(run, _key_tiles(jq), jq))
        elif n not in shared:
            shared.add(n)
            pl.when(nk == n)(functools.partial(run, [(t * TK, TK) for t in range(n)], j))


def _admissible(tile, jq):
    key = lax.broadcasted_iota(I32, (tile[1], TQ), 0) + tile[0]
    qpos = lax.broadcasted_iota(I32, (1, TQ), 1) + jq * TQ
    return key < ((qpos >> 6) + 1) * CHUNK


def _dsa_kernel(seq_len, ki_ref, ka_ref, vat_ref, qit_ref, wit_ref, qat_ref, gat_ref, out_ref,
                sc_sc, hi_sc, lo_sc, s_sc):
    topk = min(TOPK_MAX, seq_len // 4)
    idx_scale = (IDX_DIM * HEADS) ** -0.5
    neg_hi, neg_lo = NEG_KEY >> 16, (NEG_KEY & 0xFFFF) - 32768

    def run(tiles, jq):
        n_keys = tiles[-1][0] + tiles[-1][1]
        n_extra = seq_len - n_keys
        idx_bits = (n_keys - 1).bit_length()
        zeros64 = jnp.zeros((64, TQ), BF16)

        def key_index(tile):
            return lax.broadcasted_iota(I32, (tile[1], TQ), 0) + tile[0]

        qi = qit_ref[0]
        w = wit_ref[0]
        qpads = [jnp.concatenate([qi[h * 64:(h + 1) * 64], zeros64], axis=0) for h in range(HEADS)]
        for tile in tiles:
            kt_tile = ki_ref[0, _rows(tile), :]
            acc = jnp.zeros((tile[1], TQ), F32)
            for h in range(HEADS):
                rel = jnp.dot(kt_tile, qpads[h], preferred_element_type=F32)
                acc = acc + jnp.maximum(rel, 0.0) * w[h:h + 1, :]
            score = acc * idx_scale
            if tile is tiles[-1]:
                score = jnp.where(_admissible(tile, jq), score, NEG_INF)
            sc_sc[_rows(tile), :] = score
            bits = lax.bitcast_convert_type(score, I32)
            key = jnp.where(bits < 0, INT_MIN - bits, bits)
            hi_sc[_rows(tile), :] = (key >> 16).astype(I16)
            lo_sc[_rows(tile), :] = ((key & 0xFFFF) - 32768).astype(I16)

        def count(pred):
            part = jnp.zeros((8, TQ), I32)
            for tile in tiles:
                hit = pred(sc_sc[_rows(tile), :], tile).astype(I32)
                part = part + jnp.sum(hit.reshape(tile[1] // 8, 8, TQ), axis=0)
            return jnp.sum(part, axis=0, keepdims=True)

        def count_ge(cand):
            return count(lambda s, tile: s >= cand) + jnp.where(NEG_INF >= cand, n_extra, 0)

        def count16(ref, pred):
            parts = []
            for tile in tiles:
                hit = jnp.where(pred(ref[_rows(tile), :]), jnp.int16(1), jnp.int16(0))
                parts += [hit[r * 16:(r + 1) * 16] for r in range(tile[1] // 16)]
            while len(parts) > 1:
                odd = parts[-1:] if len(parts) % 2 else []
                parts = [a + b for a, b in zip(parts[0::2], parts[1::2])] + odd
            return jnp.sum(parts[0].astype(I32), axis=0, keepdims=True)

        def kth_largest16(ref, need, extra_ge):
            def bit_body(i, t_u):
                c_u = t_u | jnp.left_shift(jnp.int32(1), 15 - i)
                c_s = c_u - 32768
                c16 = c_s.astype(I16)
                cnt = count16(ref, lambda x: x >= c16) + extra_ge(c_s)
                return jnp.where(cnt >= need, c_u, t_u)
            return lax.fori_loop(0, 16, bit_body, jnp.zeros((1, TQ), I32)) - 32768

        t_hi = kth_largest16(hi_sc, topk, lambda c: jnp.where(neg_hi >= c, n_extra, 0))
        t_hi16 = t_hi.astype(I16)
        above = count16(hi_sc, lambda x: x > t_hi16) + jnp.where(neg_hi > t_hi, n_extra, 0)
        for tile in tiles:
            lo_sc[_rows(tile), :] = jnp.where(hi_sc[_rows(tile), :] == t_hi16, lo_sc[_rows(tile), :],
                                              jnp.int16(-32768))
        t_lo = kth_largest16(
            lo_sc, topk - above,
            lambda c: jnp.where(jnp.logical_and(t_hi == neg_hi, neg_lo >= c), n_extra, 0))
        key_fast = t_hi * 65536 + (t_lo + 32768)

        cnt_fast = count_ge(_key_to_float(key_fast))
        cnt_next = count_ge(_key_to_float(_next_key_up(key_fast)))
        certified = jnp.min(jnp.where(jnp.logical_and(cnt_fast >= topk, cnt_next < topk), 1, 0)) > 0

        def float_search():
            def bit_body(i, t_u):
                c_u = t_u | jnp.left_shift(jnp.int32(1), 31 - i)
                return jnp.where(count_ge(_key_to_float(c_u ^ INT_MIN)) >= topk, c_u, t_u)
            key = lax.fori_loop(0, 32, bit_body, jnp.zeros((1, TQ), I32)) ^ INT_MIN
            return key, count_ge(_key_to_float(key))

        thr_key, cnt_ge = lax.cond(certified, lambda: (key_fast, cnt_fast), float_search)
        thr = _key_to_float(thr_key)
        has_tie = jnp.max(jnp.where(cnt_ge != topk, 1, 0)) > 0

        def tie_limit():
            cnt_gt = count(lambda s, tile: s > thr) + jnp.where(NEG_INF > thr, n_extra, 0)
            need = topk - cnt_gt

            def idx_body(i, p):
                c = p | jnp.left_shift(jnp.int32(1), idx_bits - 1 - i)
                below = count(lambda s, tile: jnp.logical_and(s == thr, key_index(tile) < c))
                return jnp.where(below < need, c, p)

            return lax.fori_loop(0, idx_bits, idx_body, jnp.zeros((1, TQ), I32))

        p_lim = lax.cond(has_tie, tie_limit, lambda: jnp.full((1, TQ), seq_len, I32))

        qa = qat_ref[0]
        qa_pads = []
        for h in range(HEADS):
            qh = qa[h * 64:(h + 1) * 64]
            qa_pads.append(jnp.concatenate([zeros64, qh] if h % 2 else [qh, zeros64], axis=0))

        def bias_fn(tile):
            s = sc_sc[_rows(tile), :]
            sel = jnp.logical_or(s > thr, jnp.logical_and(s == thr, key_index(tile) <= p_lim))
            if tile is tiles[-1]:
                sel = jnp.logical_and(sel, _admissible(tile, jq))
            return jnp.where(sel, 0.0, -jnp.inf)

        def qk_fn(h, tile):
            kpair = ka_ref[0, _rows(tile), (h // 2) * 128:(h // 2 + 1) * 128]
            return jnp.dot(kpair, qa_pads[h], preferred_element_type=F32)

        m = _logits_pass(tiles, qk_fn, bias_fn, s_sc)
        _exp_pv(tiles, m, vat_ref, gat_ref, out_ref, s_sc)

    _for_each_key_range(seq_len, run)


def _mla_kernel(seq_len, kb_ref, vbt_ref, qbt_ref, gbt_ref, out_ref, s_sc):
    def run(jq):
        tiles = _key_tiles(jq)
        qb = qbt_ref[0]

        def bias_fn(tile):
            return jnp.where(_admissible(tile, jq), 0.0, -jnp.inf) if tile is tiles[-1] else None

        def qk_fn(h, tile):
            kh = kb_ref[0, _rows(tile), h * B_PAD:(h + 1) * B_PAD]
            return jnp.dot(kh, qb[h * B_PAD:(h + 1) * B_PAD], preferred_element_type=F32)

        m = _logits_pass(tiles, qk_fn, bias_fn, s_sc)
        _exp_pv(tiles, m, vbt_ref, gbt_ref, out_ref, s_sc)

    _for_each_query_block(seq_len, run)


def _out_kernel(x_ref, mat_ref, mbt_ref, w_ref, out_ref):
    mixed_t = jnp.concatenate([mat_ref[0], mbt_ref[0]], axis=0)
    y = lax.dot_general(mixed_t, w_ref[...], (((0,), (0,)), ((), ())),
                        preferred_element_type=F32)
    out_ref[0] = x_ref[0] + y


def _rope_tables_t(seq_len, dim):
    half = dim // 2
    freqs = jnp.power(ROPE_THETA, -jnp.arange(half, dtype=F32) * 2.0 / dim)
    ang = jnp.arange(seq_len, dtype=jnp.int32).astype(F32)[None, :] * freqs[:, None]
    return jnp.cos(ang), jnp.sin(ang)


def _params(sem):
    return pltpu.CompilerParams(dimension_semantics=sem, vmem_limit_bytes=V7X_VMEM_LIMIT)


def _layer(h, norm_gain, w_in, a_q_norm, a_k_norm, b_q_latent_norm, b_kv_latent_norm,
           w_uq, w_ukv, b_q_norm, b_k_norm, w_out):
    B, S, D = h.shape
    assert D == D_MODEL and S % TM == 0 and S % TO == 0 and TM % TK == 0 and TK % TQ == 0 and S <= 2048
    nst, nkt, nqb = S // TM, S // TK, S // TQ
    vext = HEADS * V_EXT

    w_t = w_in.T.astype(BF16)
    wuq_t = jnp.pad(w_uq.reshape(Q_LORA, HEADS, B_QK), ((0, 0), (0, 0), (0, B_PAD - B_QK)))
    wuq_t = wuq_t.reshape(Q_LORA, HEADS * B_PAD).T.astype(BF16)
    wukv = w_ukv.reshape(KV_LORA, HEADS, B_NOPE + HEAD_DIM)
    wuk_t = wukv[:, :, :B_NOPE].reshape(KV_LORA, WIDTH).T.astype(BF16)
    wuv_t = wukv[:, :, B_NOPE:].reshape(KV_LORA, WIDTH).T.astype(BF16)
    lanes = lambda g: jnp.broadcast_to(g.astype(F32)[:, None], (g.shape[0], TM))
    pad_b = lambda g: jnp.pad(g, (0, B_PAD - B_QK))
    c64, s64 = _rope_tables_t(S, HEAD_DIM)
    c32, s32 = _rope_tables_t(S, B_ROPE)

    const = lambda shape: pl.BlockSpec(shape, lambda b, i: (0,) * len(shape))
    tok_t = lambda c: pl.BlockSpec((1, c, TM), lambda b, i: (b, 0, i))
    tok = lambda c: pl.BlockSpec((1, TM, c), lambda b, i: (b, i, 0))
    tiles = lambda c: pl.BlockSpec((1, TM // TK, c, TK), lambda b, i: (b, i, 0, 0))
    tab = lambda r: pl.BlockSpec((r, TM), lambda b, i: (0, i))
    sds = jax.ShapeDtypeStruct

    (qat, ka, vat, gat, qit, ki, wit, qbt, kb, vbt, gbt) = pl.pallas_call(
        _proj_kernel,
        grid=(B, nst),
        in_specs=[tok(D), const((1, D)), const((O_END, D)), const((HEADS * B_PAD, Q_LORA)),
                  const((WIDTH, KV_LORA)), const((WIDTH, KV_LORA)),
                  const((64, TM)), const((64, TM)), const((Q_LORA, TM)), const((KV_LORA, TM)),
                  const((B_PAD, TM)), const((B_PAD, TM)),
                  tab(32), tab(32), tab(16), tab(16)],
        out_specs=[tok_t(WIDTH), tok(WIDTH), tiles(vext), tok_t(WIDTH), tok_t(WIDTH), tok(128),
                   tok_t(HEADS), tok_t(HEADS * B_PAD), tok(HEADS * B_PAD), tiles(vext), tok_t(WIDTH)],
        out_shape=[sds((B, WIDTH, S), BF16), sds((B, S, WIDTH), BF16), sds((B, nkt, vext, TK), BF16),
                   sds((B, WIDTH, S), BF16), sds((B, WIDTH, S), BF16), sds((B, S, 128), BF16),
                   sds((B, HEADS, S), F32), sds((B, HEADS * B_PAD, S), BF16),
                   sds((B, S, HEADS * B_PAD), BF16), sds((B, nkt, vext, TK), BF16),
                   sds((B, WIDTH, S), BF16)],
        compiler_params=_params(("arbitrary", "arbitrary")),
        name="proj",
    )(h, norm_gain.reshape(1, D), w_t, wuq_t, wuk_t, wuv_t,
      lanes(a_q_norm), lanes(a_k_norm), lanes(b_q_latent_norm), lanes(b_kv_latent_norm),
      lanes(pad_b(b_q_norm)), lanes(pad_b(b_k_norm)), c64, s64, c32, s32)

    per_b = lambda *shape: pl.BlockSpec((1,) + shape, lambda b, j: (b,) + (0,) * len(shape))
    qblk = lambda c: pl.BlockSpec((1, c, TQ), lambda b, j: (b, 0, j))
    attn_scratch = [pltpu.VMEM((HEADS, S, TQ), F32)]

    mixed_a = pl.pallas_call(
        functools.partial(_dsa_kernel, S),
        grid=(B, nqb),
        in_specs=[per_b(S, 128), per_b(S, WIDTH), per_b(nkt, vext, TK),
                  qblk(WIDTH), qblk(HEADS), qblk(WIDTH), qblk(WIDTH)],
        out_specs=qblk(WIDTH),
        out_shape=sds((B, WIDTH, S), BF16),
        scratch_shapes=[pltpu.VMEM((S, TQ), F32), pltpu.VMEM((S, TQ), I16),
                        pltpu.VMEM((S, TQ), I16)] + attn_scratch,
        compiler_params=_params(("arbitrary", "arbitrary")),
        name="dsa",
    )(ki, ka, vat, qit, wit, qat, gat)

    mixed_b = pl.pallas_call(
        functools.partial(_mla_kernel, S),
        grid=(B, nqb),
        in_specs=[per_b(S, HEADS * B_PAD), per_b(nkt, vext, TK), qblk(HEADS * B_PAD), qblk(WIDTH)],
        out_specs=qblk(WIDTH),
        out_shape=sds((B, WIDTH, S), BF16),
        scratch_shapes=attn_scratch,
        compiler_params=_params(("arbitrary", "arbitrary")),
        name="mla",
    )(kb, vbt, qbt, gbt)

    return pl.pallas_call(
        _out_kernel,
        grid=(B, S // TO),
        in_specs=[pl.BlockSpec((1, TO, D), lambda b, i: (b, i, 0)),
                  pl.BlockSpec((1, WIDTH, TO), lambda b, i: (b, 0, i)),
                  pl.BlockSpec((1, WIDTH, TO), lambda b, i: (b, 0, i)), const((2 * WIDTH, D))],
        out_specs=pl.BlockSpec((1, TO, D), lambda b, i: (b, i, 0)),
        out_shape=sds((B, S, D), F32),
        compiler_params=_params(("arbitrary", "arbitrary")),
        name="outproj",
    )(h, mixed_a, mixed_b, w_out.astype(BF16))


def kernel(x, norm_gain, w_in, a_q_norm, a_k_norm, b_q_latent_norm, b_kv_latent_norm,
           w_uq, w_ukv, b_q_norm, b_k_norm, w_out):
    h = x
    for l in range(norm_gain.shape[0]):
        h = _layer(h, norm_gain[l], w_in[l], a_q_norm[l], a_k_norm[l], b_q_latent_norm[l],
                   b_kv_latent_norm[l], w_uq[l], w_ukv[l], b_q_norm[l], b_k_norm[l], w_out[l])
    return h
```

```python
import functools

import numpy as np
import jax
import jax.numpy as jnp
from jax import lax
from jax.experimental import pallas as pl
from jax.experimental.pallas import tpu as pltpu

F32 = jnp.float32
BF16 = jnp.bfloat16
I32 = jnp.int32
I16 = jnp.int16

D_MODEL = 1024
CHUNK = 64
ROPE_THETA = 10000.0
RMS_EPS = 1e-6
NEG_INF = -1e30
HEADS = 8
HEAD_DIM = 64
WIDTH = HEADS * HEAD_DIM
IDX_DIM = 64
IDX_ROPE_DIM = 32
TOPK_MAX = 256
B_NOPE = 64
B_ROPE = 32
B_QK = B_NOPE + B_ROPE
B_PAD = 128
Q_LORA = 384
KV_LORA = 256
V_EXT = 80

O_QA, O_KA, O_VA, O_GA = 0, 512, 1024, 1536
O_QI, O_KI, O_WI = 2048, 2560, 2624
O_CQ, O_CKV, O_KR, O_GB, O_END = 2632, 3016, 3272, 3304, 3816

TM = 512
TO = 1024
TQ = 256
TK = 512
V7X_VMEM_LIMIT = 56 * 1024 * 1024
DSA_EXACT_TILES = 2

INT_MIN = -(2 ** 31)
F32_LOWEST = float(np.finfo(np.float32).min)
LOG2E = float(np.log2(np.e))
KEY_POS_INF = 0x7F800000
KEY_NEG_INF = INT_MIN + 0x00800000
KEY_MIN_NORMAL = 0x00800000
_NEG_BITS = int(np.array(NEG_INF, np.float32).view(np.int32))
NEG_KEY = int(np.int32(np.int64(INT_MIN) - np.int64(_NEG_BITS)))


def _rope_rows(a, b, cos, sin):
    return a * cos - b * sin, b * cos + a * sin


def _head_norm_rope_t(xh, gain, cos, sin, n_true, rope_lo, half):
    ssq = jnp.sum(xh * xh, axis=0, keepdims=True)
    y = (xh * lax.rsqrt(ssq * (1.0 / n_true) + RMS_EPS)) * gain
    ra, rb = _rope_rows(y[rope_lo:rope_lo + half], y[rope_lo + half:rope_lo + 2 * half], cos, sin)
    parts = []
    if rope_lo:
        parts.append(y[:rope_lo])
    parts += [ra, rb]
    if rope_lo + 2 * half < y.shape[0]:
        parts.append(y[rope_lo + 2 * half:])
    return jnp.concatenate(parts, axis=0)


def _silu(g):
    return g * (1.0 / (1.0 + jnp.exp(-g)))


def _key_to_float(key):
    key = jnp.clip(key, KEY_NEG_INF, KEY_POS_INF)
    key = jnp.where(jnp.abs(key) < KEY_MIN_NORMAL, 0, key)
    return lax.bitcast_convert_type(jnp.where(key < 0, INT_MIN - key, key), F32)


def _next_key_up(key):
    nxt = jnp.minimum(key, KEY_POS_INF) + 1
    return jnp.where(jnp.logical_and(nxt > 0, nxt < KEY_MIN_NORMAL), KEY_MIN_NORMAL, nxt)


def _proj_kernel(x_ref, ng_ref, w_ref, wuq_ref, wuk_ref, wuv_ref,
                 gaq_ref, gak_ref, gql_ref, gkvl_ref, gbq_ref, gbk_ref,
                 c64_ref, s64_ref, c32_ref, s32_ref,
                 qat_ref, ka_ref, vat_ref, gat_ref, qit_ref, ki_ref, wit_ref,
                 qbt_ref, kb_ref, vbt_ref, gbt_ref):
    x = x_ref[0]
    ms = jnp.mean(x * x, axis=-1, keepdims=True)
    xb = ((x * lax.rsqrt(ms + RMS_EPS)) * ng_ref[...]).astype(BF16)

    def proj_t(lo, hi):
        return lax.dot_general(w_ref[lo:hi, :], xb, (((1,), (1,)), ((), ())),
                               preferred_element_type=F32)

    c64, s64 = c64_ref[...], s64_ref[...]
    c32, s32 = c32_ref[...], s32_ref[...]
    nkt = TM // TK
    ones_rows = jnp.ones((V_EXT - HEAD_DIM, TM), F32)

    def store_tiles(ref, val):
        ext = jnp.concatenate(
            [r for h in range(HEADS) for r in (val[h * 64:(h + 1) * 64], ones_rows)], axis=0).astype(BF16)
        for t in range(nkt):
            ref[0, t] = ext[:, t * TK:(t + 1) * TK]

    qk = proj_t(O_QA, O_VA)
    vg = proj_t(O_VA, O_QI)
    mid = proj_t(O_QI, O_GB)
    gb = proj_t(O_GB, O_END)
    qa = qk[:WIDTH]
    gaq = gaq_ref[...]
    qa = jnp.concatenate(
        [_head_norm_rope_t(qa[h * 64:(h + 1) * 64], gaq, c64, s64, 64, 0, 32) for h in range(HEADS)], axis=0)
    qat_ref[0] = (qa * (HEAD_DIM ** -0.5 * LOG2E)).astype(BF16)

    ka = qk[WIDTH:]
    gak = gak_ref[...]
    ka = jnp.concatenate(
        [_head_norm_rope_t(ka[h * 64:(h + 1) * 64], gak, c64, s64, 64, 0, 32) for h in range(HEADS)], axis=0)
    ka_ref[0] = ka.T.astype(BF16)

    store_tiles(vat_ref, vg[:WIDTH])
    gat_ref[0] = _silu(vg[WIDTH:]).astype(BF16)

    qi = mid[:O_KI - O_QI]
    qi_parts = []
    for h in range(HEADS):
        qh = qi[h * 64:(h + 1) * 64]
        ra, rb = _rope_rows(qh[0:16], qh[16:32], c32, s32)
        qi_parts += [ra, rb, qh[32:]]
    qit_ref[0] = jnp.concatenate(qi_parts, axis=0).astype(BF16)

    kw = mid[O_KI - O_QI:O_CQ - O_QI]
    ra, rb = _rope_rows(kw[0:16], kw[16:32], c32, s32)
    ki = jnp.concatenate([ra, rb, kw[32:64], jnp.zeros((64, TM), F32)], axis=0)
    ki_ref[0] = ki.T.astype(BF16)
    wit_ref[0] = kw[64:72]

    lat = mid[O_CQ - O_QI:]
    cq = lat[:Q_LORA]
    cq = (cq * lax.rsqrt(jnp.mean(cq * cq, axis=0, keepdims=True) + RMS_EPS)) * gql_ref[...]
    qb = jnp.dot(wuq_ref[...], cq.astype(BF16), preferred_element_type=F32)
    gbq = gbq_ref[...]
    qb = jnp.concatenate(
        [_head_norm_rope_t(qb[h * B_PAD:(h + 1) * B_PAD], gbq, c32, s32, B_QK, B_NOPE, 16)
         for h in range(HEADS)], axis=0)
    qbt_ref[0] = (qb * (B_QK ** -0.5 * LOG2E)).astype(BF16)

    ckv = lat[Q_LORA:Q_LORA + KV_LORA]
    ckv = ((ckv * lax.rsqrt(jnp.mean(ckv * ckv, axis=0, keepdims=True) + RMS_EPS)) * gkvl_ref[...]).astype(BF16)
    kr = lat[Q_LORA + KV_LORA:]
    kn = jnp.dot(wuk_ref[...], ckv, preferred_element_type=F32)
    gbk = gbk_ref[...]
    zpad = jnp.zeros((B_PAD - B_QK, TM), F32)
    kb = jnp.concatenate(
        [_head_norm_rope_t(jnp.concatenate([kn[h * 64:(h + 1) * 64], kr, zpad], axis=0),
                           gbk, c32, s32, B_QK, B_NOPE, 16) for h in range(HEADS)], axis=0)
    kb_ref[0] = kb.T.astype(BF16)
    vb = jnp.dot(wuv_ref[...], ckv, preferred_element_type=F32)
    store_tiles(vbt_ref, vb)

    gbt_ref[0] = _silu(gb).astype(BF16)


def _key_tiles(jq):
    n_keys = (jq + 1) * TQ
    tiles = [(t * TK, TK) for t in range(n_keys // TK)]
    if n_keys % TK:
        tiles.append((n_keys - n_keys % TK, n_keys % TK))
    return tiles


def _rows(tile):
    return slice(tile[0], tile[0] + tile[1])


def _logits_pass(tiles, qk_fn, bias_fn, s_sc):
    mrun = [jnp.full((8, TQ), F32_LOWEST, F32) for _ in range(HEADS)]
    for tile in tiles:
        bias = bias_fn(tile)
        for h in range(HEADS):
            s = qk_fn(h, tile)
            if bias is not None:
                s = s + bias
            s_sc[h, _rows(tile), :] = s
            mrun[h] = jnp.maximum(mrun[h], jnp.max(s.reshape(tile[1] // 8, 8, TQ), axis=0))
    return [jnp.max(mrun[h], axis=0, keepdims=True) for h in range(HEADS)]


def _exp_pv(tiles, m, vt_ref, gate_ref, out_ref, s_sc):
    for h in range(HEADS):
        acc = jnp.zeros((V_EXT, TQ), F32)
        for start, size in tiles:
            p = jnp.exp2((s_sc[h, start:start + size, :] - m[h]).astype(BF16))
            acc = acc + jnp.dot(vt_ref[0, start // TK, h * V_EXT:(h + 1) * V_EXT, :size], p,
                                preferred_element_type=F32)
        o = acc[:HEAD_DIM] * (1.0 / acc[HEAD_DIM:HEAD_DIM + 1])
        g = gate_ref[0, h * 64:(h + 1) * 64, :].astype(F32)
        out_ref[0, h * 64:(h + 1) * 64, :] = (o * g).astype(BF16)


def _for_each_query_block(seq_len, run):
    j = pl.program_id(1)
    for jq in range(seq_len // TQ):
        pl.when(j == jq)(functools.partial(run, jq))


def _for_each_key_range(seq_len, run):
    j = pl.program_id(1)
    nk = (j * TQ) // TK + 1
    shared = set()
    for jq in range(seq_len // TQ):
        n = (jq * TQ) // TK + 1
        if n <= DSA_EXACT_TILES:
            pl.when(j == jq)(functools.partial(run, _key_tiles(jq), jq))
        elif n not in shared:
            shared.add(n)
            pl.when(nk == n)(functools.partial(run, [(t * TK, TK) for t in range(n)], j))


def _admissible(tile, jq):
    key = lax.broadcasted_iota(I32, (tile[1], TQ), 0) + tile[0]
    qpos = lax.broadcasted_iota(I32, (1, TQ), 1) + jq * TQ
    return key < ((qpos >> 6) + 1) * CHUNK


def _dsa_kernel(seq_len, ki_ref, ka_ref, vat_ref, qit_ref, wit_ref, qat_ref, gat_ref, out_ref,
                sc_sc, hi_sc, lo_sc, s_sc):
    topk = min(TOPK_MAX, seq_len // 4)
    idx_scale = (IDX_DIM * HEADS) ** -0.5
    neg_hi, neg_lo = NEG_KEY >> 16, (NEG_KEY & 0xFFFF) - 32768

    def run(tiles, jq):
        n_keys = tiles[-1][0] + tiles[-1][1]
        n_extra = seq_len - n_keys
        idx_bits = (n_keys - 1).bit_length()
        zeros64 = jnp.zeros((64, TQ), BF16)

        def key_index(tile):
            return lax.broadcasted_iota(I32, (tile[1], TQ), 0) + tile[0]

        qi = qit_ref[0]
        w = wit_ref[0]
        qpads = [jnp.concatenate([qi[h * 64:(h + 1) * 64], zeros64], axis=0) for h in range(HEADS)]
        for tile in tiles:
            kt_tile = ki_ref[0, _rows(tile), :]
            acc = jnp.zeros((tile[1], TQ), F32)
            for h in range(HEADS):
                rel = jnp.dot(kt_tile, qpads[h], preferred_element_type=F32)
                acc = acc + jnp.maximum(rel, 0.0) * w[h:h + 1, :]
            score = acc * idx_scale
            if tile is tiles[-1]:
                score = jnp.where(_admissible(tile, jq), score, NEG_INF)
            sc_sc[_rows(tile), :] = score
            bits = lax.bitcast_convert_type(score, I32)
            key = jnp.where(bits < 0, INT_MIN - bits, bits)
            hi_sc[_rows(tile), :] = (key >> 16).astype(I16)
            lo_sc[_rows(tile), :] = ((key & 0xFFFF) - 32768).astype(I16)

        def count(pred):
            part = jnp.zeros((8, TQ), I32)
            for tile in tiles:
                hit = pred(sc_sc[_rows(tile), :], tile).astype(I32)
                part = part + jnp.sum(hit.reshape(tile[1] // 8, 8, TQ), axis=0)
            return jnp.sum(part, axis=0, keepdims=True)

        def count_ge(cand):
            return count(lambda s, tile: s >= cand) + jnp.where(NEG_INF >= cand, n_extra, 0)

        def count16(ref, pred):
            parts = []
            for tile in tiles:
                hit = jnp.where(pred(ref[_rows(tile), :]), jnp.int16(1), jnp.int16(0))
                parts += [hit[r * 16:(r + 1) * 16] for r in range(tile[1] // 16)]
            while len(parts) > 1:
                odd = parts[-1:] if len(parts) % 2 else []
                parts = [a + b for a, b in zip(parts[0::2], parts[1::2])] + odd
            return jnp.sum(parts[0].astype(I32), axis=0, keepdims=True)

        def kth_largest16(ref, need, extra_ge):
            def bit_body(i, t_u):
                c_u = t_u | jnp.left_shift(jnp.int32(1), 15 - i)
                c_s = c_u - 32768
                c16 = c_s.astype(I16)
                cnt = count16(ref, lambda x: x >= c16) + extra_ge(c_s)
                return jnp.where(cnt >= need, c_u, t_u)
            return lax.fori_loop(0, 16, bit_body, jnp.zeros((1, TQ), I32)) - 32768

        t_hi = kth_largest16(hi_sc, topk, lambda c: jnp.where(neg_hi >= c, n_extra, 0))
        t_hi16 = t_hi.astype(I16)
        above = count16(hi_sc, lambda x: x > t_hi16) + jnp.where(neg_hi > t_hi, n_extra, 0)
        for tile in tiles:
            lo_sc[_rows(tile), :] = jnp.where(hi_sc[_rows(tile), :] == t_hi16, lo_sc[_rows(tile), :],
                                              jnp.int16(-32768))
        t_lo = kth_largest16(
            lo_sc, topk - above,
            lambda c: jnp.where(jnp.logical_and(t_hi == neg_hi, neg_lo >= c), n_extra, 0))
        key_fast = t_hi * 65536 + (t_lo + 32768)

        cnt_fast = count_ge(_key_to_float(key_fast))
        cnt_next = count_ge(_key_to_float(_next_key_up(key_fast)))
        certified = jnp.min(jnp.where(jnp.logical_and(cnt_fast >= topk, cnt_next < topk), 1, 0)) > 0

        def float_search():
            def bit_body(i, t_u):
                c_u = t_u | jnp.left_shift(jnp.int32(1), 31 - i)
                return jnp.where(count_ge(_key_to_float(c_u ^ INT_MIN)) >= topk, c_u, t_u)
            key = lax.fori_loop(0, 32, bit_body, jnp.zeros((1, TQ), I32)) ^ INT_MIN
            return key, count_ge(_key_to_float(key))

        thr_key, cnt_ge = lax.cond(certified, lambda: (key_fast, cnt_fast), float_search)
        thr = _key_to_float(thr_key)
        has_tie = jnp.max(jnp.where(cnt_ge != topk, 1, 0)) > 0

        def tie_limit():
            cnt_gt = count(lambda s, tile: s > thr) + jnp.where(NEG_INF > thr, n_extra, 0)
            need = topk - cnt_gt

            def idx_body(i, p):
                c = p | jnp.left_shift(jnp.int32(1), idx_bits - 1 - i)
                below = count(lambda s, tile: jnp.logical_and(s == thr, key_index(tile) < c))
                return jnp.where(below < need, c, p)

            return lax.fori_loop(0, idx_bits, idx_body, jnp.zeros((1, TQ), I32))

        p_lim = lax.cond(has_tie, tie_limit, lambda: jnp.full((1, TQ), seq_len, I32))

        qa = qat_ref[0]
        qa_pads = []
        for h in range(HEADS):
            qh = qa[h * 64:(h + 1) * 64]
            qa_pads.append(jnp.concatenate([zeros64, qh] if h % 2 else [qh, zeros64], axis=0))

        def bias_fn(tile):
            s = sc_sc[_rows(tile), :]
            sel = jnp.logical_or(s > thr, jnp.logical_and(s == thr, key_index(tile) <= p_lim))
            if tile is tiles[-1]:
                sel = jnp.logical_and(sel, _admissible(tile, jq))
            return jnp.where(sel, 0.0, -jnp.inf)

        def qk_fn(h, tile):
            kpair = ka_ref[0, _rows(tile), (h // 2) * 128:(h // 2 + 1) * 128]
            return jnp.dot(kpair, qa_pads[h], preferred_element_type=F32)

        m = _logits_pass(tiles, qk_fn, bias_fn, s_sc)
        _exp_pv(tiles, m, vat_ref, gat_ref, out_ref, s_sc)

    _for_each_key_range(seq_len, run)


def _mla_kernel(seq_len, kb_ref, vbt_ref, qbt_ref, gbt_ref, out_ref, s_sc):
    def run(jq):
        tiles = _key_tiles(jq)
        qb = qbt_ref[0]

        def bias_fn(tile):
            return jnp.where(_admissible(tile, jq), 0.0, -jnp.inf) if tile is tiles[-1] else None

        def qk_fn(h, tile):
            kh = kb_ref[0, _rows(tile), h * B_PAD:(h + 1) * B_PAD]
            return jnp.dot(kh, qb[h * B_PAD:(h + 1) * B_PAD], preferred_element_type=F32)

        m = _logits_pass(tiles, qk_fn, bias_fn, s_sc)
        _exp_pv(tiles, m, vbt_ref, gbt_ref, out_ref, s_sc)

    _for_each_query_block(seq_len, run)


def _out_kernel(x_ref, mat_ref, mbt_ref, w_ref, out_ref):
    mixed_t = jnp.concatenate([mat_ref[0], mbt_ref[0]], axis=0)
    y = lax.dot_general(mixed_t, w_ref[...], (((0,), (0,)), ((), ())),
                        preferred_element_type=F32)
    out_ref[0] = x_ref[0] + y


def _rope_tables_t(seq_len, dim):
    half = dim // 2
    freqs = jnp.power(ROPE_THETA, -jnp.arange(half, dtype=F32) * 2.0 / dim)
    ang = jnp.arange(seq_len, dtype=jnp.int32).astype(F32)[None, :] * freqs[:, None]
    return jnp.cos(ang), jnp.sin(ang)


def _params(sem):
    return pltpu.CompilerParams(dimension_semantics=sem, vmem_limit_bytes=V7X_VMEM_LIMIT)


def _layer(h, norm_gain, w_in, a_q_norm, a_k_norm, b_q_latent_norm, b_kv_latent_norm,
           w_uq, w_ukv, b_q_norm, b_k_norm, w_out):
    B, S, D = h.shape
    assert D == D_MODEL and S % TM == 0 and S % TO == 0 and TM % TK == 0 and TK % TQ == 0 and S <= 2048
    nst, nkt, nqb = S // TM, S // TK, S // TQ
    vext = HEADS * V_EXT

    w_t = w_in.T.astype(BF16)
    wuq_t = jnp.pad(w_uq.reshape(Q_LORA, HEADS, B_QK), ((0, 0), (0, 0), (0, B_PAD - B_QK)))
    wuq_t = wuq_t.reshape(Q_LORA, HEADS * B_PAD).T.astype(BF16)
    wukv = w_ukv.reshape(KV_LORA, HEADS, B_NOPE + HEAD_DIM)
    wuk_t = wukv[:, :, :B_NOPE].reshape(KV_LORA, WIDTH).T.astype(BF16)
    wuv_t = wukv[:, :, B_NOPE:].reshape(KV_LORA, WIDTH).T.astype(BF16)
    lanes = lambda g: jnp.broadcast_to(g.astype(F32)[:, None], (g.shape[0], TM))
    pad_b = lambda g: jnp.pad(g, (0, B_PAD - B_QK))
    c64, s64 = _rope_tables_t(S, HEAD_DIM)
    c32, s32 = _rope_tables_t(S, B_ROPE)

    const = lambda shape: pl.BlockSpec(shape, lambda b, i: (0,) * len(shape))
    tok_t = lambda c: pl.BlockSpec((1, c, TM), lambda b, i: (b, 0, i))
    tok = lambda c: pl.BlockSpec((1, TM, c), lambda b, i: (b, i, 0))
    tiles = lambda c: pl.BlockSpec((1, TM // TK, c, TK), lambda b, i: (b, i, 0, 0))
    tab = lambda r: pl.BlockSpec((r, TM), lambda b, i: (0, i))
    sds = jax.ShapeDtypeStruct

    (qat, ka, vat, gat, qit, ki, wit, qbt, kb, vbt, gbt) = pl.pallas_call(
        _proj_kernel,
        grid=(B, nst),
        in_specs=[tok(D), const((1, D)), const((O_END, D)), const((HEADS * B_PAD, Q_LORA)),
                  const((WIDTH, KV_LORA)), const((WIDTH, KV_LORA)),
                  const((64, TM)), const((64, TM)), const((Q_LORA, TM)), const((KV_LORA, TM)),
                  const((B_PAD, TM)), const((B_PAD, TM)),
                  tab(32), tab(32), tab(16), tab(16)],
        out_specs=[tok_t(WIDTH), tok(WIDTH), tiles(vext), tok_t(WIDTH), tok_t(WIDTH), tok(128),
                   tok_t(HEADS), tok_t(HEADS * B_PAD), tok(HEADS * B_PAD), tiles(vext), tok_t(WIDTH)],
        out_shape=[sds((B, WIDTH, S), BF16), sds((B, S, WIDTH), BF16), sds((B, nkt, vext, TK), BF16),
                   sds((B, WIDTH, S), BF16), sds((B, WIDTH, S), BF16), sds((B, S, 128), BF16),
                   sds((B, HEADS, S), F32), sds((B, HEADS * B_PAD, S), BF16),
                   sds((B, S, HEADS * B_PAD), BF16), sds((B, nkt, vext, TK), BF16),
                   sds((B, WIDTH, S), BF16)],
        compiler_params=_params(("arbitrary", "arbitrary")),
        name="proj",
    )(h, norm_gain.reshape(1, D), w_t, wuq_t, wuk_t, wuv_t,
      lanes(a_q_norm), lanes(a_k_norm), lanes(b_q_latent_norm), lanes(b_kv_latent_norm),
      lanes(pad_b(b_q_norm)), lanes(pad_b(b_k_norm)), c64, s64, c32, s32)

    per_b = lambda *shape: pl.BlockSpec((1,) + shape, lambda b, j: (b,) + (0,) * len(shape))
    qblk = lambda c: pl.BlockSpec((1, c, TQ), lambda b, j: (b, 0, j))
    attn_scratch = [pltpu.VMEM((HEADS, S, TQ), F32)]

    mixed_a = pl.pallas_call(
        functools.partial(_dsa_kernel, S),
        grid=(B, nqb),
        in_specs=[per_b(S, 128), per_b(S, WIDTH), per_b(nkt, vext, TK),
                  qblk(WIDTH), qblk(HEADS), qblk(WIDTH), qblk(WIDTH)],
        out_specs=qblk(WIDTH),
        out_shape=sds((B, WIDTH, S), BF16),
        scratch_shapes=[pltpu.VMEM((S, TQ), F32), pltpu.VMEM((S, TQ), I16),
                        pltpu.VMEM((S, TQ), I16)] + attn_scratch,
        compiler_params=_params(("arbitrary", "arbitrary")),
        name="dsa",
    )(ki, ka, vat, qit, wit, qat, gat)

    mixed_b = pl.pallas_call(
        functools.partial(_mla_kernel, S),
        grid=(B, nqb),
        in_specs=[per_b(S, HEADS * B_PAD), per_b(nkt, vext, TK), qblk(HEADS * B_PAD), qblk(WIDTH)],
        out_specs=qblk(WIDTH),
        out_shape=sds((B, WIDTH, S), BF16),
        scratch_shapes=attn_scratch,
        compiler_params=_params(("arbitrary", "arbitrary")),
        name="mla",
    )(kb, vbt, qbt, gbt)

    return pl.pallas_call(
        _out_kernel,
        grid=(B, S // TO),
        in_specs=[pl.BlockSpec((1, TO, D), lambda b, i: (b, i, 0)),
                  pl.BlockSpec((1, WIDTH, TO), lambda b, i: (b, 0, i)),
                  pl.BlockSpec((1, WIDTH, TO), lambda b, i: (b, 0, i)), const((2 * WIDTH, D))],
        out_specs=pl.BlockSpec((1, TO, D), lambda b, i: (b, i, 0)),
        out_shape=sds((B, S, D), F32),
        compiler_params=_params(("arbitrary", "arbitrary")),
        name="outproj",
    )(h, mixed_a, mixed_b, w_out.astype(BF16))


def kernel(x, norm_gain, w_in, a_q_norm, a_k_norm, b_q_latent_norm, b_kv_latent_norm,
           w_uq, w_ukv, b_q_norm, b_k_norm, w_out):
    h = x
    for l in range(norm_gain.shape[0]):
        h = _layer(h, norm_gain[l], w_in[l], a_q_norm[l], a_k_norm[l], b_q_latent_norm[l],
                   b_kv_latent_norm[l], w_uq[l], w_ukv[l], b_q_norm[l], b_k_norm[l], w_out[l])
    return h
```

```python
import functools

import numpy as np
import jax
import jax.numpy as jnp
from jax import lax
from jax.experimental import pallas as pl
from jax.experimental.pallas import tpu as pltpu

F32 = jnp.float32
BF16 = jnp.bfloat16
I32 = jnp.int32
I16 = jnp.int16

D_MODEL = 1024
CHUNK = 64
ROPE_THETA = 10000.0
RMS_EPS = 1e-6
NEG_INF = -1e30
HEADS = 8
HEAD_DIM = 64
WIDTH = HEADS * HEAD_DIM
IDX_DIM = 64
IDX_ROPE_DIM = 32
TOPK_MAX = 256
B_NOPE = 64
B_ROPE = 32
B_QK = B_NOPE + B_ROPE
B_PAD = 128
Q_LORA = 384
KV_LORA = 256
V_EXT = 80

O_QA, O_KA, O_VA, O_GA = 0, 512, 1024, 1536
O_QI, O_KI, O_WI = 2048, 2560, 2624
O_CQ, O_CKV, O_KR, O_GB, O_END = 2632, 3016, 3272, 3304, 3816

TM = 512
TQ = 256
TK = 512
V7X_VMEM_LIMIT = 56 * 1024 * 1024
DSA_EXACT_TILES = 2

INT_MIN = -(2 ** 31)
F32_LOWEST = float(np.finfo(np.float32).min)
LOG2E = float(np.log2(np.e))
KEY_POS_INF = 0x7F800000
KEY_NEG_INF = INT_MIN + 0x00800000
KEY_MIN_NORMAL = 0x00800000
_NEG_BITS = int(np.array(NEG_INF, np.float32).view(np.int32))
NEG_KEY = int(np.int32(np.int64(INT_MIN) - np.int64(_NEG_BITS)))


def _rope_rows(a, b, cos, sin):
    return a * cos - b * sin, b * cos + a * sin


def _head_norm_rope_t(xh, gain, cos, sin, n_true, rope_lo, half):
    ssq = jnp.sum(xh * xh, axis=0, keepdims=True)
    y = (xh * lax.rsqrt(ssq * (1.0 / n_true) + RMS_EPS)) * gain
    ra, rb = _rope_rows(y[rope_lo:rope_lo + half], y[rope_lo + half:rope_lo + 2 * half], cos, sin)
    parts = []
    if rope_lo:
        parts.append(y[:rope_lo])
    parts += [ra, rb]
    if rope_lo + 2 * half < y.shape[0]:
        parts.append(y[rope_lo + 2 * half:])
    return jnp.concatenate(parts, axis=0)


def _silu(g):
    return g * (1.0 / (1.0 + jnp.exp(-g)))


def _key_to_float(key):
    key = jnp.clip(key, KEY_NEG_INF, KEY_POS_INF)
    key = jnp.where(jnp.abs(key) < KEY_MIN_NORMAL, 0, key)
    return lax.bitcast_convert_type(jnp.where(key < 0, INT_MIN - key, key), F32)


def _next_key_up(key):
    nxt = jnp.minimum(key, KEY_POS_INF) + 1
    return jnp.where(jnp.logical_and(nxt > 0, nxt < KEY_MIN_NORMAL), KEY_MIN_NORMAL, nxt)


def _proj_kernel(x_ref, ng_ref, w_ref, wuq_ref, wuk_ref, wuv_ref,
                 gaq_ref, gak_ref, gql_ref, gkvl_ref, gbq_ref, gbk_ref,
                 c64_ref, s64_ref, c32_ref, s32_ref,
                 qat_ref, ka_ref, vat_ref, gat_ref, qit_ref, ki_ref, wit_ref,
                 qbt_ref, kb_ref, vbt_ref, gbt_ref):
    x = x_ref[0]
    ms = jnp.mean(x * x, axis=-1, keepdims=True)
    xb = ((x * lax.rsqrt(ms + RMS_EPS)) * ng_ref[...]).astype(BF16)

    def proj_t(lo, hi):
        return lax.dot_general(w_ref[lo:hi, :], xb, (((1,), (1,)), ((), ())),
                               preferred_element_type=F32)

    c64, s64 = c64_ref[...], s64_ref[...]
    c32, s32 = c32_ref[...], s32_ref[...]
    nkt = TM // TK
    ones_rows = jnp.ones((V_EXT - HEAD_DIM, TM), F32)

    def store_tiles(ref, val):
        ext = jnp.concatenate(
            [r for h in range(HEADS) for r in (val[h * 64:(h + 1) * 64], ones_rows)], axis=0).astype(BF16)
        for t in range(nkt):
            ref[0, t] = ext[:, t * TK:(t + 1) * TK]

    qk = proj_t(O_QA, O_VA)
    vg = proj_t(O_VA, O_QI)
    mid = proj_t(O_QI, O_GB)
    gb = proj_t(O_GB, O_END)
    qa = qk[:WIDTH]
    gaq = gaq_ref[...]
    qa = jnp.concatenate(
        [_head_norm_rope_t(qa[h * 64:(h + 1) * 64], gaq, c64, s64, 64, 0, 32) for h in range(HEADS)], axis=0)
    qat_ref[0] = (qa * (HEAD_DIM ** -0.5 * LOG2E)).astype(BF16)

    ka = qk[WIDTH:]
    gak = gak_ref[...]
    ka = jnp.concatenate(
        [_head_norm_rope_t(ka[h * 64:(h + 1) * 64], gak, c64, s64, 64, 0, 32) for h in range(HEADS)], axis=0)
    ka_ref[0] = ka.T.astype(BF16)

    store_tiles(vat_ref, vg[:WIDTH])
    gat_ref[0] = _silu(vg[WIDTH:]).astype(BF16)

    qi = mid[:O_KI - O_QI]
    qi_parts = []
    for h in range(HEADS):
        qh = qi[h * 64:(h + 1) * 64]
        ra, rb = _rope_rows(qh[0:16], qh[16:32], c32, s32)
        qi_parts += [ra, rb, qh[32:]]
    qit_ref[0] = jnp.concatenate(qi_parts, axis=0).astype(BF16)

    kw = mid[O_KI - O_QI:O_CQ - O_QI]
    ra, rb = _rope_rows(kw[0:16], kw[16:32], c32, s32)
    ki = jnp.concatenate([ra, rb, kw[32:64], jnp.zeros((64, TM), F32)], axis=0)
    ki_ref[0] = ki.T.astype(BF16)
    wit_ref[0] = kw[64:72]

    lat = mid[O_CQ - O_QI:]
    cq = lat[:Q_LORA]
    cq = (cq * lax.rsqrt(jnp.mean(cq * cq, axis=0, keepdims=True) + RMS_EPS)) * gql_ref[...]
    qb = jnp.dot(wuq_ref[...], cq.astype(BF16), preferred_element_type=F32)
    gbq = gbq_ref[...]
    qb = jnp.concatenate(
        [_head_norm_rope_t(qb[h * B_PAD:(h + 1) * B_PAD], gbq, c32, s32, B_QK, B_NOPE, 16)
         for h in range(HEADS)], axis=0)
    qbt_ref[0] = (qb * (B_QK ** -0.5 * LOG2E)).astype(BF16)

    ckv = lat[Q_LORA:Q_LORA + KV_LORA]
    ckv = ((ckv * lax.rsqrt(jnp.mean(ckv * ckv, axis=0, keepdims=True) + RMS_EPS)) * gkvl_ref[...]).astype(BF16)
    kr = lat[Q_LORA + KV_LORA:]
    kn = jnp.dot(wuk_ref[...], ckv, preferred_element_type=F32)
    gbk = gbk_ref[...]
    zpad = jnp.zeros((B_PAD - B_QK, TM), F32)
    kb = jnp.concatenate(
        [_head_norm_rope_t(jnp.concatenate([kn[h * 64:(h + 1) * 64], kr, zpad], axis=0),
                           gbk, c32, s32, B_QK, B_NOPE, 16) for h in range(HEADS)], axis=0)
    kb_ref[0] = kb.T.astype(BF16)
    vb = jnp.dot(wuv_ref[...], ckv, preferred_element_type=F32)
    store_tiles(vbt_ref, vb)

    gbt_ref[0] = _silu(gb).astype(BF16)


def _key_tiles(jq):
    n_keys = (jq + 1) * TQ
    tiles = [(t * TK, TK) for t in range(n_keys // TK)]
    if n_keys % TK:
        tiles.append((n_keys - n_keys % TK, n_keys % TK))
    return tiles


def _rows(tile):
    return slice(tile[0], tile[0] + tile[1])


def _logits_pass(tiles, qk_fn, bias_fn, s_sc):
    mrun = [jnp.full((8, TQ), F32_LOWEST, F32) for _ in range(HEADS)]
    for tile in tiles:
        bias = bias_fn(tile)
        for h in range(HEADS):
            s = qk_fn(h, tile)
            if bias is not None:
                s = s + bias
            s_sc[h, _rows(tile), :] = s
            mrun[h] = jnp.maximum(mrun[h], jnp.max(s.reshape(tile[1] // 8, 8, TQ), axis=0))
    return [jnp.max(mrun[h], axis=0, keepdims=True) for h in range(HEADS)]


def _exp_pv(tiles, m, vt_ref, gate_ref, out_ref, s_sc):
    for h in range(HEADS):
        acc = jnp.zeros((V_EXT, TQ), F32)
        for start, size in tiles:
            p = jnp.exp2((s_sc[h, start:start + size, :] - m[h]).astype(BF16))
            acc = acc + jnp.dot(vt_ref[0, start // TK, h * V_EXT:(h + 1) * V_EXT, :size], p,
                                preferred_element_type=F32)
        o = acc[:HEAD_DIM] * (1.0 / acc[HEAD_DIM:HEAD_DIM + 1])
        g = gate_ref[0, h * 64:(h + 1) * 64, :].astype(F32)
        out_ref[0, h * 64:(h + 1) * 64, :] = (o * g).astype(BF16)


def _for_each_query_block(seq_len, run):
    j = pl.program_id(1)
    for jq in range(seq_len // TQ):
        pl.when(j == jq)(functools.partial(run, jq))


def _for_each_key_range(seq_len, run):
    j = pl.program_id(1)
    nk = (j * TQ) // TK + 1
    shared = set()
    for jq in range(seq_len // TQ):
        n = (jq * TQ) // TK + 1
        if n <= DSA_EXACT_TILES:
            pl.when(j == jq)(functools.partial(run, _key_tiles(jq), jq))
        elif n not in shared:
            shared.add(n)
            pl.when(nk == n)(functools.partial(run, [(t * TK, TK) for t in range(n)], j))


def _admissible(tile, jq):
    key = lax.broadcasted_iota(I32, (tile[1], TQ), 0) + tile[0]
    qpos = lax.broadcasted_iota(I32, (1, TQ), 1) + jq * TQ
    return key < ((qpos >> 6) + 1) * CHUNK


def _dsa_kernel(seq_len, ki_ref, ka_ref, vat_ref, qit_ref, wit_ref, qat_ref, gat_ref, out_ref,
                sc_sc, hi_sc, lo_sc, s_sc):
    topk = min(TOPK_MAX, seq_len // 4)
    idx_scale = (IDX_DIM * HEADS) ** -0.5
    neg_hi, neg_lo = NEG_KEY >> 16, (NEG_KEY & 0xFFFF) - 32768

    def run(tiles, jq):
        n_keys = tiles[-1][0] + tiles[-1][1]
        n_extra = seq_len - n_keys
        idx_bits = (n_keys - 1).bit_length()
        zeros64 = jnp.zeros((64, TQ), BF16)

        def key_index(tile):
            return lax.broadcasted_iota(I32, (tile[1], TQ), 0) + tile[0]

        qi = qit_ref[0]
        w = wit_ref[0]
        qpads = [jnp.concatenate([qi[h * 64:(h + 1) * 64], zeros64], axis=0) for h in range(HEADS)]
        for tile in tiles:
            kt_tile = ki_ref[0, _rows(tile), :]
            acc = jnp.zeros((tile[1], TQ), F32)
            for h in range(HEADS):
                rel = jnp.dot(kt_tile, qpads[h], preferred_element_type=F32)
                acc = acc + jnp.maximum(rel, 0.0) * w[h:h + 1, :]
            score = acc * idx_scale
            if tile is tiles[-1]:
                score = jnp.where(_admissible(tile, jq), score, NEG_INF)
            sc_sc[_rows(tile), :] = score
            bits = lax.bitcast_convert_type(score, I32)
            key = jnp.where(bits < 0, INT_MIN - bits, bits)
            hi_sc[_rows(tile), :] = (key >> 16).astype(I16)
            lo_sc[_rows(tile), :] = ((key & 0xFFFF) - 32768).astype(I16)

        def count(pred):
            part = jnp.zeros((8, TQ), I32)
            for tile in tiles:
                hit = pred(sc_sc[_rows(tile), :], tile).astype(I32)
                part = part + jnp.sum(hit.reshape(tile[1] // 8, 8, TQ), axis=0)
            return jnp.sum(part, axis=0, keepdims=True)

        def count_ge(cand):
            return count(lambda s, tile: s >= cand) + jnp.where(NEG_INF >= cand, n_extra, 0)

        def count16(ref, pred):
            parts = []
            for tile in tiles:
                hit = jnp.where(pred(ref[_rows(tile), :]), jnp.int16(1), jnp.int16(0))
                parts += [hit[r * 16:(r + 1) * 16] for r in range(tile[1] // 16)]
            while len(parts) > 1:
                odd = parts[-1:] if len(parts) % 2 else []
                parts = [a + b for a, b in zip(parts[0::2], parts[1::2])] + odd
            return jnp.sum(parts[0].astype(I32), axis=0, keepdims=True)

        def kth_largest16(ref, need, extra_ge):
            def bit_body(i, t_u):
                c_u = t_u | jnp.left_shift(jnp.int32(1), 15 - i)
                c_s = c_u - 32768
                c16 = c_s.astype(I16)
                cnt = count16(ref, lambda x: x >= c16) + extra_ge(c_s)
                return jnp.where(cnt >= need, c_u, t_u)
            return lax.fori_loop(0, 16, bit_body, jnp.zeros((1, TQ), I32)) - 32768

        t_hi = kth_largest16(hi_sc, topk, lambda c: jnp.where(neg_hi >= c, n_extra, 0))
        t_hi16 = t_hi.astype(I16)
        above = count16(hi_sc, lambda x: x > t_hi16) + jnp.where(neg_hi > t_hi, n_extra, 0)
        for tile in tiles:
            lo_sc[_rows(tile), :] = jnp.where(hi_sc[_rows(tile), :] == t_hi16, lo_sc[_rows(tile), :],
                                              jnp.int16(-32768))
        t_lo = kth_largest16(
            lo_sc, topk - above,
            lambda c: jnp.where(jnp.logical_and(t_hi == neg_hi, neg_lo >= c), n_extra, 0))
        key_fast = t_hi * 65536 + (t_lo + 32768)

        cnt_fast = count_ge(_key_to_float(key_fast))
        cnt_next = count_ge(_key_to_float(_next_key_up(key_fast)))
        certified = jnp.min(jnp.where(jnp.logical_and(cnt_fast >= topk, cnt_next < topk), 1, 0)) > 0

        def float_search():
            def bit_body(i, t_u):
                c_u = t_u | jnp.left_shift(jnp.int32(1), 31 - i)
                return jnp.where(count_ge(_key_to_float(c_u ^ INT_MIN)) >= topk, c_u, t_u)
            key = lax.fori_loop(0, 32, bit_body, jnp.zeros((1, TQ), I32)) ^ INT_MIN
            return key, count_ge(_key_to_float(key))

        thr_key, cnt_ge = lax.cond(certified, lambda: (key_fast, cnt_fast), float_search)
        thr = _key_to_float(thr_key)
        has_tie = jnp.max(jnp.where(cnt_ge != topk, 1, 0)) > 0

        def tie_limit():
            cnt_gt = count(lambda s, tile: s > thr) + jnp.where(NEG_INF > thr, n_extra, 0)
            need = topk - cnt_gt

            def idx_body(i, p):
                c = p | jnp.left_shift(jnp.int32(1), idx_bits - 1 - i)
                below = count(lambda s, tile: jnp.logical_and(s == thr, key_index(tile) < c))
                return jnp.where(below < need, c, p)

            return lax.fori_loop(0, idx_bits, idx_body, jnp.zeros((1, TQ), I32))

        p_lim = lax.cond(has_tie, tie_limit, lambda: jnp.full((1, TQ), seq_len, I32))

        qa = qat_ref[0]
        qa_pads = []
        for h in range(HEADS):
            qh = qa[h * 64:(h + 1) * 64]
            qa_pads.append(jnp.concatenate([zeros64, qh] if h % 2 else [qh, zeros64], axis=0))

        def bias_fn(tile):
            s = sc_sc[_rows(tile), :]
            sel = jnp.logical_or(s > thr, jnp.logical_and(s == thr, key_index(tile) <= p_lim))
            if tile is tiles[-1]:
                sel = jnp.logical_and(sel, _admissible(tile, jq))
            return jnp.where(sel, 0.0, -jnp.inf)

        def qk_fn(h, tile):
            kpair = ka_ref[0, _rows(tile), (h // 2) * 128:(h // 2 + 1) * 128]
            return jnp.dot(kpair, qa_pads[h], preferred_element_type=F32)

        m = _logits_pass(tiles, qk_fn, bias_fn, s_sc)
        _exp_pv(tiles, m, vat_ref, gat_ref, out_ref, s_sc)

    _for_each_key_range(seq_len, run)


def _mla_kernel(seq_len, kb_ref, vbt_ref, qbt_ref, gbt_ref, x_ref, mat_ref, w_ref, out_ref,
                s_sc, mb_sc):
    def run(jq):
        tiles = _key_tiles(jq)
        qb = qbt_ref[0]

        def bias_fn(tile):
            return jnp.where(_admissible(tile, jq), 0.0, -jnp.inf) if tile is tiles[-1] else None

        def qk_fn(h, tile):
            kh = kb_ref[0, _rows(tile), h * B_PAD:(h + 1) * B_PAD]
            return jnp.dot(kh, qb[h * B_PAD:(h + 1) * B_PAD], preferred_element_type=F32)

        m = _logits_pass(tiles, qk_fn, bias_fn, s_sc)
        _exp_pv(tiles, m, vbt_ref, gbt_ref, mb_sc, s_sc)

    _for_each_query_block(seq_len, run)
    mixed_t = jnp.concatenate([mat_ref[0], mb_sc[0]], axis=0)
    y = lax.dot_general(mixed_t, w_ref[...], (((0,), (0,)), ((), ())),
                        preferred_element_type=F32)
    out_ref[0] = x_ref[0] + y


def _rope_tables_t(seq_len, dim):
    half = dim // 2
    freqs = jnp.power(ROPE_THETA, -jnp.arange(half, dtype=F32) * 2.0 / dim)
    ang = jnp.arange(seq_len, dtype=jnp.int32).astype(F32)[None, :] * freqs[:, None]
    return jnp.cos(ang), jnp.sin(ang)


def _params(sem):
    return pltpu.CompilerParams(dimension_semantics=sem, vmem_limit_bytes=V7X_VMEM_LIMIT)


def _layer(h, norm_gain, w_in, a_q_norm, a_k_norm, b_q_latent_norm, b_kv_latent_norm,
           w_uq, w_ukv, b_q_norm, b_k_norm, w_out):
    B, S, D = h.shape
    assert D == D_MODEL and S % TM == 0 and TM % TK == 0 and TK % TQ == 0 and S <= 2048
    nst, nkt, nqb = S // TM, S // TK, S // TQ
    vext = HEADS * V_EXT

    w_t = w_in.T.astype(BF16)
    wuq_t = jnp.pad(w_uq.reshape(Q_LORA, HEADS, B_QK), ((0, 0), (0, 0), (0, B_PAD - B_QK)))
    wuq_t = wuq_t.reshape(Q_LORA, HEADS * B_PAD).T.astype(BF16)
    wukv = w_ukv.reshape(KV_LORA, HEADS, B_NOPE + HEAD_DIM)
    wuk_t = wukv[:, :, :B_NOPE].reshape(KV_LORA, WIDTH).T.astype(BF16)
    wuv_t = wukv[:, :, B_NOPE:].reshape(KV_LORA, WIDTH).T.astype(BF16)
    lanes = lambda g: jnp.broadcast_to(g.astype(F32)[:, None], (g.shape[0], TM))
    pad_b = lambda g: jnp.pad(g, (0, B_PAD - B_QK))
    c64, s64 = _rope_tables_t(S, HEAD_DIM)
    c32, s32 = _rope_tables_t(S, B_ROPE)

    const = lambda shape: pl.BlockSpec(shape, lambda b, i: (0,) * len(shape))
    tok_t = lambda c: pl.BlockSpec((1, c, TM), lambda b, i: (b, 0, i))
    tok = lambda c: pl.BlockSpec((1, TM, c), lambda b, i: (b, i, 0))
    tiles = lambda c: pl.BlockSpec((1, TM // TK, c, TK), lambda b, i: (b, i, 0, 0))
    tab = lambda r: pl.BlockSpec((r, TM), lambda b, i: (0, i))
    sds = jax.ShapeDtypeStruct

    (qat, ka, vat, gat, qit, ki, wit, qbt, kb, vbt, gbt) = pl.pallas_call(
        _proj_kernel,
        grid=(B, nst),
        in_specs=[tok(D), const((1, D)), const((O_END, D)), const((HEADS * B_PAD, Q_LORA)),
                  const((WIDTH, KV_LORA)), const((WIDTH, KV_LORA)),
                  const((64, TM)), const((64, TM)), const((Q_LORA, TM)), const((KV_LORA, TM)),
                  const((B_PAD, TM)), const((B_PAD, TM)),
                  tab(32), tab(32), tab(16), tab(16)],
        out_specs=[tok_t(WIDTH), tok(WIDTH), tiles(vext), tok_t(WIDTH), tok_t(WIDTH), tok(128),
                   tok_t(HEADS), tok_t(HEADS * B_PAD), tok(HEADS * B_PAD), tiles(vext), tok_t(WIDTH)],
        out_shape=[sds((B, WIDTH, S), BF16), sds((B, S, WIDTH), BF16), sds((B, nkt, vext, TK), BF16),
                   sds((B, WIDTH, S), BF16), sds((B, WIDTH, S), BF16), sds((B, S, 128), BF16),
                   sds((B, HEADS, S), F32), sds((B, HEADS * B_PAD, S), BF16),
                   sds((B, S, HEADS * B_PAD), BF16), sds((B, nkt, vext, TK), BF16),
                   sds((B, WIDTH, S), BF16)],
        compiler_params=_params(("arbitrary", "arbitrary")),
        name="proj",
    )(h, norm_gain.reshape(1, D), w_t, wuq_t, wuk_t, wuv_t,
      lanes(a_q_norm), lanes(a_k_norm), lanes(b_q_latent_norm), lanes(b_kv_latent_norm),
      lanes(pad_b(b_q_norm)), lanes(pad_b(b_k_norm)), c64, s64, c32, s32)

    per_b = lambda *shape: pl.BlockSpec((1,) + shape, lambda b, j: (b,) + (0,) * len(shape))
    qblk = lambda c: pl.BlockSpec((1, c, TQ), lambda b, j: (b, 0, j))
    attn_scratch = [pltpu.VMEM((HEADS, S, TQ), F32)]

    mixed_a = pl.pallas_call(
        functools.partial(_dsa_kernel, S),
        grid=(B, nqb),
        in_specs=[per_b(S, 128), per_b(S, WIDTH), per_b(nkt, vext, TK),
                  qblk(WIDTH), qblk(HEADS), qblk(WIDTH), qblk(WIDTH)],
        out_specs=qblk(WIDTH),
        out_shape=sds((B, WIDTH, S), BF16),
        scratch_shapes=[pltpu.VMEM((S, TQ), F32), pltpu.VMEM((S, TQ), I16),
                        pltpu.VMEM((S, TQ), I16)] + attn_scratch,
        compiler_params=_params(("arbitrary", "arbitrary")),
        name="dsa",
    )(ki, ka, vat, qit, wit, qat, gat)

    return pl.pallas_call(
        functools.partial(_mla_kernel, S),
        grid=(B, nqb),
        in_specs=[per_b(S, HEADS * B_PAD), per_b(nkt, vext, TK), qblk(HEADS * B_PAD), qblk(WIDTH),
                  pl.BlockSpec((1, TQ, D), lambda b, j: (b, j, 0)), qblk(WIDTH),
                  pl.BlockSpec((2 * WIDTH, D), lambda b, j: (0, 0))],
        out_specs=pl.BlockSpec((1, TQ, D), lambda b, j: (b, j, 0)),
        out_shape=sds((B, S, D), F32),
        scratch_shapes=attn_scratch + [pltpu.VMEM((1, WIDTH, TQ), BF16)],
        compiler_params=_params(("arbitrary", "arbitrary")),
        name="mla",
    )(kb, vbt, qbt, gbt, h, mixed_a, w_out.astype(BF16))


def kernel(x, norm_gain, w_in, a_q_norm, a_k_norm, b_q_latent_norm, b_kv_latent_norm,
           w_uq, w_ukv, b_q_norm, b_k_norm, w_out):
    h = x
    for l in range(norm_gain.shape[0]):
        h = _layer(h, norm_gain[l], w_in[l], a_q_norm[l], a_k_norm[l], b_q_latent_norm[l],
                   b_kv_latent_norm[l], w_uq[l], w_ukv[l], b_q_norm[l], b_k_norm[l], w_out[l])
    return h
```
